```python
import jax
import jax.numpy as jnp
from jax import lax
import numpy as np

D_MODEL = 1024
BATCH = 32
SEQ = 2048
DEPTH = 4
DEC_BATCH = 16
DEC_SEQ = 16
PAST_LEN = 4096

CHUNK = 64
QBLK = 128
N_MIXERS = 2
N_MLA = (DEPTH + 1) // 2
N_RET = DEPTH // 2
MLA_HEADS = D_MODEL // 128
QK_NOPE = 128
QK_ROPE = 64
V_HEAD = 128
Q_LORA = 3 * D_MODEL // 8
KV_LORA = D_MODEL // 4
MLA_SCALE = (QK_NOPE + QK_ROPE) ** -0.5
RET_HEADS = 4
RET_DK = D_MODEL // RET_HEADS
RET_DV = 2 * D_MODEL // RET_HEADS
FFN_DIM = 256 * ((8 * D_MODEL // 3 + 255) // 256)
EPS = 1e-6
ROPE_THETA = 10000.0

kernel_name = 'mla_retention_macaron_stream_step'


def rmsnorm(x, g):
    xf = x.astype(jnp.float32)
    y = xf * lax.rsqrt(jnp.mean(xf * xf, axis=-1, keepdims=True) + EPS)
    return (y * g.astype(jnp.float32)).astype(x.dtype)


def rope(x, pos):
    d = x.shape[-1]
    inv = 1.0 / (ROPE_THETA ** (jnp.arange(0, d, 2, dtype=jnp.float32) / d))
    ang = pos.astype(jnp.float32)[:, None] * inv[None, :]
    ang = ang.reshape(ang.shape[:1] + (1,) * (x.ndim - 3) + ang.shape[1:])
    cos, sin = jnp.cos(ang), jnp.sin(ang)
    x1, x2 = x[..., : d // 2], x[..., d // 2:]
    return jnp.concatenate([x1 * cos - x2 * sin, x1 * sin + x2 * cos], axis=-1).astype(x.dtype)


def swiglu(h, wg, wu, wd):
    return (jax.nn.silu(h @ wg) * (h @ wu)) @ wd


def ffn_half(x, g, wg, wu, wd):
    return x + 0.5 * swiglu(rmsnorm(x, g), wg, wu, wd)


def mla_project(h, pos, w_dq, q_norm, w_uq, w_dkv, kv_norm):
    b, s, _ = h.shape
    cq = rmsnorm(h @ w_dq, q_norm)
    q = (cq @ w_uq).reshape(b, s, MLA_HEADS, QK_NOPE + QK_ROPE)
    q_nope, q_pe = q[..., :QK_NOPE], rope(q[..., QK_NOPE:], pos)
    kv = h @ w_dkv
    ckv = rmsnorm(kv[..., :KV_LORA], kv_norm)
    kpe = rope(kv[..., KV_LORA:], pos)
    return q_nope, q_pe, ckv, kpe


def mla_prompt(h, pos, w_dq, q_norm, w_uq, w_dkv, kv_norm, w_uk, w_uv, w_o):
    b, s, _ = h.shape
    q_nope, q_pe, ckv, kpe = mla_project(h, pos, w_dq, q_norm, w_uq, w_dkv, kv_norm)
    k_nope = jnp.einsum('bkl,lhn->bkhn', ckv, w_uk)
    v = jnp.einsum('bkl,lhv->bkhv', ckv, w_uv)
    outs = []
    for blk in range(s // QBLK):
        q0 = blk * QBLK
        kend = q0 + QBLK
        sc = (jnp.einsum('bqhn,bkhn->bhqk', q_nope[:, q0:kend], k_nope[:, :kend])
              + jnp.einsum('bqhr,bkr->bhqk', q_pe[:, q0:kend], kpe[:, :kend]))
        sc = sc.astype(jnp.float32) * MLA_SCALE
        q_chunk = (q0 + jnp.arange(QBLK)) // CHUNK
        k_chunk = jnp.arange(kend) // CHUNK
        mask = k_chunk[None, :] <= q_chunk[:, None]
        p = jax.nn.softmax(jnp.where(mask, sc, -jnp.inf), axis=-1).astype(v.dtype)
        outs.append(jnp.einsum('bhqk,bkhv->bqhv', p, v[:, :kend]))
    o = jnp.concatenate(outs, axis=1).reshape(b, s, MLA_HEADS * V_HEAD)
    return (o @ w_o).astype(h.dtype), ckv, kpe


def mla_sample(h, pos, c_ckv, c_kpe, w_dq, q_norm, w_uq, w_dkv, kv_norm, w_uk, w_uv, w_o):
    b, s, _ = h.shape
    past = c_ckv.shape[1]
    q_nope, q_pe, ckv, kpe = mla_project(h, pos, w_dq, q_norm, w_uq, w_dkv, kv_norm)
    q_lat = jnp.einsum('bqhn,lhn->bqhl', q_nope, w_uk)
    s_past = (jnp.einsum('bqhl,bkl->bhqk', q_lat, c_ckv)
              + jnp.einsum('bqhr,bkr->bhqk', q_pe, c_kpe))
    s_new = (jnp.einsum('bqhl,bkl->bhqk', q_lat, ckv)
             + jnp.einsum('bqhr,bkr->bhqk', q_pe, kpe))
    sc = jnp.concatenate([s_past, s_new], axis=-1).astype(jnp.float32) * MLA_SCALE
    p = jax.nn.softmax(sc, axis=-1).astype(ckv.dtype)
    o_lat = (jnp.einsum('bhqk,bkl->bqhl', p[..., :past], c_ckv)
             + jnp.einsum('bhqk,bkl->bqhl', p[..., past:], ckv))
    o = jnp.einsum('bqhl,lhv->bqhv', o_lat, w_uv).reshape(b, s, MLA_HEADS * V_HEAD)
    return (o @ w_o).astype(h.dtype), ckv, kpe


def ret_log_gamma():
    return jnp.log(1.0 - 2.0 ** (-5.0 - jnp.arange(RET_HEADS, dtype=jnp.float32)))


def ret_project(h, pos, w_in):
    b, s, _ = h.shape
    qk_w = RET_HEADS * RET_DK
    v_w = RET_HEADS * RET_DV
    q, k, v, g = jnp.split(h @ w_in, [qk_w, 2 * qk_w, 2 * qk_w + v_w], axis=-1)
    q = rope(q.reshape(b, s, RET_HEADS, RET_DK), pos)
    k = rope(k.reshape(b, s, RET_HEADS, RET_DK), pos) * (RET_DK ** -0.5)
    v = v.reshape(b, s, RET_HEADS, RET_DV)
    return q, k, v, g


def ret_chunk(state, qkv):
    q, k, v = (t.astype(jnp.float32) for t in qkv)
    state = state.astype(jnp.float32)
    L = q.shape[1]
    lg = ret_log_gamma()
    idx = jnp.arange(L, dtype=jnp.float32)
    diff = idx[:, None] - idx[None, :]
    decay = jnp.where(diff >= 0, jnp.exp(lg[:, None, None] * jnp.maximum(diff, 0.0)), 0.0)
    qk = jnp.einsum('bihd,bjhd->bhij', q, k) * decay
    o = jnp.einsum('bhij,bjhe->bihe', qk, v)
    xi = jnp.exp(lg[None, :] * (idx[:, None] + 1.0))
    o = o + jnp.einsum('bihd,bhde->bihe', q * xi[None, :, :, None], state)
    zeta = jnp.exp(lg[None, :] * (L - 1.0 - idx[:, None]))
    new_state = (jnp.exp(lg * L)[None, :, None, None] * state
                 + jnp.einsum('bjhd,bjhe->bhde', k * zeta[None, :, :, None], v))
    return new_state, o


def ret_output(o, g, ret_gn, w_out, dtype):
    b, s = o.shape[:2]
    y = rmsnorm(o, ret_gn.reshape(RET_HEADS, RET_DV)).reshape(b, s, RET_HEADS * RET_DV)
    return ((jax.nn.silu(g.astype(jnp.float32)) * y) @ w_out).astype(dtype)


def ret_prompt(h, pos, w_in, ret_gn, w_out):
    b, s, _ = h.shape
    q, k, v, g = ret_project(h, pos, w_in)
    n = s // CHUNK

    def to_chunks(t):
        return t.reshape((b, n, CHUNK) + t.shape[2:]).swapaxes(0, 1)

    s0 = jnp.zeros((b, RET_HEADS, RET_DK, RET_DV), jnp.float32)
    s_final, o = lax.scan(ret_chunk, s0, (to_chunks(q), to_chunks(k), to_chunks(v)))
    o = o.swapaxes(0, 1).reshape(b, s, RET_HEADS, RET_DV)
    return ret_output(o, g, ret_gn, w_out, h.dtype), s_final


def ret_sample(h, pos, state, w_in, ret_gn, w_out):
    q, k, v, g = ret_project(h, pos, w_in)
    new_state, o = ret_chunk(state, (q, k, v))
    return ret_output(o, g, ret_gn, w_out, h.dtype), new_state


def setup_inputs(seed: int = 0) -> dict:
    key = jax.random.key(seed)
    ks = iter(jax.random.split(key, 40))
    D, F = D_MODEL, FFN_DIM

    def nrm(shape, scale=1.0):
        return jax.random.normal(next(ks), shape, jnp.float32) * scale

    def gain(shape):
        return 1.0 + 0.01 * jax.random.normal(next(ks), shape, jnp.float32)

    return {
        'x_prompt': nrm((BATCH, SEQ, D)),
        'x_sample': nrm((DEC_BATCH, DEC_SEQ, D)),
        'cache_ckv': nrm((N_MLA, DEC_BATCH, PAST_LEN, KV_LORA)),
        'cache_kpe': nrm((N_MLA, DEC_BATCH, PAST_LEN, QK_ROPE)),
        'state_ret': nrm((N_RET, DEC_BATCH, RET_HEADS, RET_DK, RET_DV)),
        'norm_ffn1': gain((DEPTH, D)),
        'w_ffn1_gate': nrm((DEPTH, D, F), D ** -0.5),
        'w_ffn1_up': nrm((DEPTH, D, F), D ** -0.5),
        'w_ffn1_down': nrm((DEPTH, F, D), F ** -0.5),
        'norm_mix': gain((DEPTH, D)),
        'w_dq': nrm((N_MLA, D, Q_LORA), D ** -0.5),
        'q_norm': gain((N_MLA, Q_LORA)),
        'w_uq': nrm((N_MLA, Q_LORA, MLA_HEADS * (QK_NOPE + QK_ROPE)), Q_LORA ** -0.5),
        'w_dkv': nrm((N_MLA, D, KV_LORA + QK_ROPE), D ** -0.5),
        'kv_norm': gain((N_MLA, KV_LORA)),
        'w_uk': nrm((N_MLA, KV_LORA, MLA_HEADS, QK_NOPE), KV_LORA ** -0.5),
        'w_uv': nrm((N_MLA, KV_LORA, MLA_HEADS, V_HEAD), KV_LORA ** -0.5),
        'w_o_mla': nrm((N_MLA, MLA_HEADS * V_HEAD, D), (MLA_HEADS * V_HEAD) ** -0.5),
        'w_ret_in': nrm((N_RET, D, 2 * RET_HEADS * RET_DK + 2 * RET_HEADS * RET_DV), D ** -0.5),
        'ret_gn': gain((N_RET, RET_HEADS * RET_DV)),
        'w_ret_out': nrm((N_RET, RET_HEADS * RET_DV, D), (RET_HEADS * RET_DV) ** -0.5),
        'norm_ffn2': gain((DEPTH, D)),
        'w_ffn2_gate': nrm((DEPTH, D, F), D ** -0.5),
        'w_ffn2_up': nrm((DEPTH, D, F), D ** -0.5),
        'w_ffn2_down': nrm((DEPTH, F, D), F ** -0.5),
        'norm_final': gain((D,)),
    }


def reference(x_prompt, x_sample, cache_ckv, cache_kpe, state_ret,
              norm_ffn1, w_ffn1_gate, w_ffn1_up, w_ffn1_down, norm_mix,
              w_dq, q_norm, w_uq, w_dkv, kv_norm, w_uk, w_uv, w_o_mla,
              w_ret_in, ret_gn, w_ret_out,
              norm_ffn2, w_ffn2_gate, w_ffn2_up, w_ffn2_down, norm_final):
    seq = x_prompt.shape[1]
    past = cache_ckv.shape[2]
    dseq = x_sample.shape[1]
    pos_p = jnp.arange(seq)
    pos_s = past + jnp.arange(dseq)
    xp, xs = x_prompt, x_sample
    ckv_p, kpe_p, st_p = [], [], []
    ckv_s, kpe_s, st_s = [], [], []
    for i in range(DEPTH):
        f1 = (norm_ffn1[i], w_ffn1_gate[i], w_ffn1_up[i], w_ffn1_down[i])
        xp = ffn_half(xp, *f1)
        xs = ffn_half(xs, *f1)
        hp = rmsnorm(xp, norm_mix[i])
        hs = rmsnorm(xs, norm_mix[i])
        j = i // N_MIXERS
        if i % N_MIXERS == 0:
            mw = (w_dq[j], q_norm[j], w_uq[j], w_dkv[j], kv_norm[j], w_uk[j], w_uv[j], w_o_mla[j])
            mp, c_p, r_p = mla_prompt(hp, pos_p, *mw)
            ms, c_s, r_s = mla_sample(hs, pos_s, cache_ckv[j], cache_kpe[j], *mw)
            ckv_p.append(c_p)
            kpe_p.append(r_p)
            ckv_s.append(c_s)
            kpe_s.append(r_s)
        else:
            rw = (w_ret_in[j], ret_gn[j], w_ret_out[j])
            mp, s_p = ret_prompt(hp, pos_p, *rw)
            ms, s_s = ret_sample(hs, pos_s, state_ret[j], *rw)
            st_p.append(s_p)
            st_s.append(s_s)
        xp = xp + mp
        xs = xs + ms
        f2 = (norm_ffn2[i], w_ffn2_gate[i], w_ffn2_up[i], w_ffn2_down[i])
        xp = ffn_half(xp, *f2)
        xs = ffn_half(xs, *f2)
    y_prompt = rmsnorm(xp, norm_final)
    y_sample = rmsnorm(xs, norm_final)
    return (y_prompt, y_sample,
            jnp.stack(ckv_p), jnp.stack(kpe_p), jnp.stack(st_p),
            jnp.stack(ckv_s), jnp.stack(kpe_s), jnp.stack(st_s))
```

```python
import functools
import math

import jax
import jax.numpy as jnp
from jax import lax
from jax.experimental import pallas as pl
from jax.experimental.pallas import tpu as pltpu

F32 = jnp.float32
BF16 = jnp.bfloat16

EPS = 1e-6
ROPE_THETA = 10000.0
CHUNK = 64
LANES = 128
V7X_VMEM_LIMIT_BYTES = 56 * 1024 * 1024

TOKEN_TILE = 512
ATTN_Q_TILE = 256
RET_TILE = 256
RET_SAMPLE_STREAMS = 2


def _params(n_axes):
    return pltpu.CompilerParams(dimension_semantics=("arbitrary",) * n_axes,
                                vmem_limit_bytes=V7X_VMEM_LIMIT_BYTES)


def _resident(shape):
    zeros = (0,) * len(shape)
    return pl.BlockSpec(shape, lambda *_: zeros, pipeline_mode=pl.Buffered(1))


def _dot(a, b):
    return jnp.dot(a, b, preferred_element_type=F32)


def _dot_nt(a, b):
    return lax.dot_general(a, b, (((1,), (1,)), ((), ())), preferred_element_type=F32)


def _dot_tn(a, b):
    return lax.dot_general(a, b, (((0,), (0,)), ((), ())), preferred_element_type=F32)


def _rms(x, g):
    ms = jnp.mean(x * x, axis=-1, keepdims=True)
    return x * lax.rsqrt(ms + EPS) * g


def _silu(x):
    return x * (1.0 / (1.0 + jnp.exp(-x)))


def _ffn_body(*refs, has_proj, has_final):
    it = iter(refs)
    x_ref = next(it)
    if has_proj:
        o_ref, wp_ref = next(it), next(it)
    g_ref, wg_ref, wu_ref, wd_ref = next(it), next(it), next(it), next(it)
    if has_final:
        gf_ref = next(it)
    out_ref = next(it)

    x = x_ref[...]
    if has_proj:
        x = x + _dot(o_ref[...], wp_ref[...])
    h = _rms(x, g_ref[...]).astype(BF16)
    gate = _dot(h, wg_ref[...])
    up = _dot(h, wu_ref[...])
    a = (_silu(gate) * up).astype(BF16)
    r = x + 0.5 * _dot(a, wd_ref[...])
    if has_final:
        r = _rms(r, gf_ref[...])
    out_ref[...] = r


def _ffn(x, g, wg, wu, wd, proj=None, final_g=None):
    t, d = x.shape
    f = wg.shape[1]
    tm = min(TOKEN_TILE, t)
    row = lambda i: (i, 0)
    args, specs = [x], [pl.BlockSpec((tm, d), row)]
    if proj is not None:
        o, wp = proj
        args += [o, wp]
        specs += [pl.BlockSpec((tm, o.shape[1]), row), _resident(wp.shape)]
    args += [g, wg, wu, wd]
    specs += [_resident((1, d)), _resident((d, f)), _resident((d, f)), _resident((f, d))]
    if final_g is not None:
        args.append(final_g)
        specs.append(_resident((1, d)))
    body = functools.partial(_ffn_body, has_proj=proj is not None, has_final=final_g is not None)
    return pl.pallas_call(
        body, grid=(t // tm,), in_specs=specs,
        out_specs=pl.BlockSpec((tm, d), row),
        out_shape=jax.ShapeDtypeStruct((t, d), F32),
        compiler_params=_params(1), name="ffn_half",
    )(*args)


def _mla_proj_body(x_ref, g_ref, wdq_ref, qn_ref, wuq_ref, wdkv_ref, kvn_ref, wuk_ref, wuv_ref,
                   cos_ref, sin_ref,
                   qnope_ref, qpe_ref, ckv_ref, kpe_ref, knope_ref, v_ref, kpe2_ref, *, n_nope, n_pe, lora):
    h = _rms(x_ref[...], g_ref[...]).astype(BF16)
    cq = _rms(_dot(h, wdq_ref[...]), qn_ref[...]).astype(BF16)
    q = _dot(cq, wuq_ref[...])
    cos, sin = cos_ref[...], sin_ref[...]
    reps = n_pe // LANES
    cos_q = jnp.concatenate([cos] * reps, axis=-1)
    sin_q = jnp.concatenate([sin] * reps, axis=-1)
    qnope_ref[...] = q[:, :n_nope].astype(BF16)
    qpe_ref[...] = (q[:, n_nope:n_nope + n_pe] * cos_q + q[:, n_nope + n_pe:] * sin_q).astype(BF16)
    kv = _dot(h, wdkv_ref[...])
    ckv = _rms(kv[:, :lora], kvn_ref[...])
    ckv_ref[...] = ckv
    kpe2 = kv[:, lora:lora + LANES] * cos + kv[:, lora + LANES:] * sin
    kpe_ref[...] = kpe2[:, :LANES // 2]
    kpe2_ref[...] = kpe2.astype(BF16)
    cb = ckv.astype(BF16)
    knope_ref[...] = _dot(cb, wuk_ref[...]).astype(BF16)
    v_ref[...] = _dot(cb, wuv_ref[...]).astype(BF16)


def _mla_proj(x, w, cos, sin):
    t, d = x.shape
    tm = min(TOKEN_TILE, t)
    n_pos_tiles = cos.shape[0] // tm
    n_nope, n_pe = w["n_nope"], w["n_pe"]
    lora, rope = w["kv_norm"].shape[1], LANES // 2
    n_v = w["w_uv"].shape[1]
    row = lambda i: (i, 0)
    pos = lambda i: (i % n_pos_tiles, 0)
    body = functools.partial(_mla_proj_body, n_nope=n_nope, n_pe=n_pe, lora=lora)
    return pl.pallas_call(
        body, grid=(t // tm,),
        in_specs=[pl.BlockSpec((tm, d), row), _resident((1, d)),
                  _resident(w["w_dq"].shape), _resident(w["q_norm"].shape), _resident(w["w_uq"].shape),
                  _resident(w["w_dkv"].shape), _resident(w["kv_norm"].shape),
                  _resident(w["w_uk"].shape), _resident(w["w_uv"].shape),
                  pl.BlockSpec((tm, LANES), pos), pl.BlockSpec((tm, LANES), pos)],
        out_specs=[pl.BlockSpec((tm, n_nope), row), pl.BlockSpec((tm, n_pe), row),
                   pl.BlockSpec((tm, lora), row), pl.BlockSpec((tm, rope), row),
                   pl.BlockSpec((tm, n_nope), row), pl.BlockSpec((tm, n_v), row),
                   pl.BlockSpec((tm, LANES), row)],
        out_shape=[jax.ShapeDtypeStruct((t, n_nope), BF16), jax.ShapeDtypeStruct((t, n_pe), BF16),
                   jax.ShapeDtypeStruct((t, lora), F32), jax.ShapeDtypeStruct((t, rope), F32),
                   jax.ShapeDtypeStruct((t, n_nope), BF16), jax.ShapeDtypeStruct((t, n_v), BF16),
                   jax.ShapeDtypeStruct((t, LANES), BF16)],
        compiler_params=_params(1), name="mla_proj",
    )(x, w["norm"], w["w_dq"], w["q_norm"], w["w_uq"], w["w_dkv"], w["kv_norm"], w["w_uk"], w["w_uv"],
      cos, sin)


def _mla_attn_body(qn_ref, qp_ref, kn_ref, kp_ref, v_ref, o_ref, kcat_ref, *, seq, tq, scale):
    head = pl.program_id(1)
    lane = lax.broadcasted_iota(jnp.int32, (tq, LANES), 1)
    keep = (lane >= LANES // 2).astype(jnp.int32) == head % 2
    kcat_ref[:, :LANES] = kn_ref[...]
    kcat_ref[:, LANES:] = kp_ref[...]
    qc = lax.broadcasted_iota(jnp.int32, (tq, tq), 0) // CHUNK
    kc = lax.broadcasted_iota(jnp.int32, (tq, tq), 1) // CHUNK
    visible = kc <= qc
    for qi in range(seq // tq):
        q0 = qi * tq
        qp = jnp.where(keep, qp_ref[q0:q0 + tq, :].astype(F32), 0.0).astype(BF16)
        qh = jnp.concatenate([qn_ref[q0:q0 + tq, :], qp], axis=-1)
        s_d = jnp.where(visible, _dot_nt(qh, kcat_ref[q0:q0 + tq, :]) * scale, -jnp.inf)
        m = jnp.max(s_d, axis=-1, keepdims=True)
        if qi > 0:
            s_p = _dot_nt(qh, kcat_ref[0:q0, :]) * scale
            m = jnp.maximum(m, jnp.max(s_p, axis=-1, keepdims=True))
            p_p = jnp.exp(s_p - m)
            l = jnp.sum(p_p, axis=-1, keepdims=True)
            acc = _dot(p_p.astype(BF16), v_ref[0:q0, :])
        p_d = jnp.exp(s_d - m)
        l_d = jnp.sum(p_d, axis=-1, keepdims=True)
        acc_d = _dot(p_d.astype(BF16), v_ref[q0:q0 + tq, :])
        if qi > 0:
            l, acc = l + l_d, acc + acc_d
        else:
            l, acc = l_d, acc_d
        o_ref[q0:q0 + tq, :] = (acc * (1.0 / l)).astype(BF16)


def _mla_attn(qnope, qpe, knope, kpe2, v, batch, seq, heads, scale):
    t = batch * seq
    tq = min(ATTN_Q_TILE, seq)
    body = functools.partial(_mla_attn_body, seq=seq, tq=tq, scale=scale)
    blk = (seq, LANES)
    return pl.pallas_call(
        body, grid=(batch, heads),
        in_specs=[pl.BlockSpec(blk, lambda b, h: (b, h)), pl.BlockSpec(blk, lambda b, h: (b, h // 2)),
                  pl.BlockSpec(blk, lambda b, h: (b, h)), pl.BlockSpec(blk, lambda b, h: (b, 0)),
                  pl.BlockSpec(blk, lambda b, h: (b, h))],
        out_specs=pl.BlockSpec(blk, lambda b, h: (b, h)),
        out_shape=jax.ShapeDtypeStruct((t, heads * LANES), BF16),
        scratch_shapes=[pltpu.VMEM((seq, 2 * LANES), BF16)],
        compiler_params=_params(2), name="mla_attn",
    )(qnope, qpe, knope, kpe2, v)


def _mla_sample_body(qn_ref, qp_ref, ckvn_ref, kpe2n_ref, cc_ref, ck_ref, wuk_ref, wuv_ref, o_ref,
                     *, heads, dq, scale):
    qn, qp = qn_ref[...], qp_ref[...]
    wuk, wuv = wuk_ref[...], wuv_ref[...]
    rope = LANES // 2
    qlat = jnp.concatenate(
        [_dot_nt(qn[:, LANES * h:LANES * (h + 1)], wuk[:, LANES * h:LANES * (h + 1)]) for h in range(heads)],
        axis=0).astype(BF16)
    qpe = jnp.concatenate([qp[:, rope * h:rope * (h + 1)] for h in range(heads)], axis=0)
    cc = cc_ref[...].astype(BF16)
    ck = ck_ref[...].astype(BF16)
    cn = ckvn_ref[...].astype(BF16)
    kn = kpe2n_ref[...][:, :rope]
    s_past = (_dot_nt(qlat, cc) + _dot_nt(qpe, ck)) * scale
    s_new = (_dot_nt(qlat, cn) + _dot_nt(qpe, kn)) * scale
    m = jnp.maximum(jnp.max(s_past, axis=-1, keepdims=True), jnp.max(s_new, axis=-1, keepdims=True))
    p_past = jnp.exp(s_past - m)
    p_new = jnp.exp(s_new - m)
    l = jnp.sum(p_past, axis=-1, keepdims=True) + jnp.sum(p_new, axis=-1, keepdims=True)
    o_lat = _dot(p_past.astype(BF16), cc) + _dot(p_new.astype(BF16), cn)
    ob = (o_lat * (1.0 / l)).astype(BF16)
    o = jnp.concatenate(
        [_dot(ob[dq * h:dq * (h + 1), :], wuv[:, LANES * h:LANES * (h + 1)]) for h in range(heads)], axis=-1)
    o_ref[...] = o.astype(BF16)


def _mla_sample(qnope, qpe, ckv_new, kpe2_new, cache_ckv, cache_kpe, layer, w, streams, dq, heads, scale):
    past, lora = cache_ckv.shape[2], cache_ckv.shape[3]
    rope = cache_kpe.shape[3]
    row = lambda b: (b, 0)
    body = functools.partial(_mla_sample_body, heads=heads, dq=dq, scale=scale)
    return pl.pallas_call(
        body, grid=(streams,),
        in_specs=[pl.BlockSpec((dq, qnope.shape[1]), row), pl.BlockSpec((dq, qpe.shape[1]), row),
                  pl.BlockSpec((dq, lora), row), pl.BlockSpec((dq, LANES), row),
                  pl.BlockSpec((None, None, past, lora), lambda b: (layer, b, 0, 0)),
                  pl.BlockSpec((None, None, past, rope), lambda b: (layer, b, 0, 0)),
                  _resident(w["w_uk"].shape), _resident(w["w_uv"].shape)],
        out_specs=pl.BlockSpec((dq, w["w_uv"].shape[1]), row),
        out_shape=jax.ShapeDtypeStruct((streams * dq, w["w_uv"].shape[1]), BF16),
        compiler_params=_params(1), name="mla_sample",
    )(qnope, qpe, ckv_new, kpe2_new, cache_ckv, cache_kpe, w["w_uk"], w["w_uv"])


def _ret_body(*refs, nb, ts, heads, dk, dv, has_init):
    it = iter(refs)
    x_ref, g_ref, win_ref, gn_ref = next(it), next(it), next(it), next(it)
    cos_ref, sin_ref, dec_ref, xi_ref, zeta_ref, gl_ref = (next(it), next(it), next(it), next(it), next(it),
                                                           next(it))
    if has_init:
        s0_ref = next(it)
    y_ref, st_ref = next(it), next(it)

    @pl.when(pl.program_id(1) == 0)
    def _():
        st_ref[...] = s0_ref[...] if has_init else jnp.zeros_like(st_ref)

    d = x_ref.shape[-1]
    h = _rms(x_ref[...].reshape(nb * ts, d), g_ref[...]).astype(BF16)
    qkvg = _dot(h, win_ref[...])
    cos, sin = cos_ref[...], sin_ref[...]
    half = dk // 2

    def rot(u):
        u1, u2 = u[:, :half], u[:, half:]
        return jnp.concatenate([u1 * cos - u2 * sin, u1 * sin + u2 * cos], axis=-1)

    k_off, v_off, g_off = heads * dk, 2 * heads * dk, 2 * heads * dk + heads * dv
    for bi in range(nb):
        rows = slice(bi * ts, (bi + 1) * ts)
        ys = []
        for hh in range(heads):
            q = rot(qkvg[rows, dk * hh:dk * (hh + 1)])
            k = rot(qkvg[rows, k_off + dk * hh:k_off + dk * (hh + 1)]) * (dk ** -0.5)
            v = qkvg[rows, v_off + dv * hh:v_off + dv * (hh + 1)].astype(BF16)
            gate = qkvg[rows, g_off + dv * hh:g_off + dv * (hh + 1)]
            inner = (_dot_nt(q.astype(BF16), k.astype(BF16)) * dec_ref[hh]).astype(BF16)
            state = st_ref[bi, hh]
            o = _dot(inner, v) + _dot((q * xi_ref[hh]).astype(BF16), state.astype(BF16))
            st_ref[bi, hh] = gl_ref[hh] * state + _dot_tn((k * zeta_ref[hh]).astype(BF16), v)
            ys.append((_silu(gate) * _rms(o, gn_ref[hh])).astype(BF16))
        y_ref[bi] = jnp.concatenate(ys, axis=-1)


def _ret(x, w, tables, state0, nb, ts):
    batch, seq, d = x.shape
    heads, dk, dv = w["heads"], w["dk"], w["dv"]
    cos, sin, dec, xi, zeta, gl = tables
    has_init = state0 is not None
    st_blk = pl.BlockSpec((nb, heads, dk, dv), lambda b, s: (b, 0, 0, 0))
    args = [x, w["norm"], w["w_in"], w["gn"], cos, sin, dec, xi, zeta, gl]
    specs = [pl.BlockSpec((nb, ts, d), lambda b, s: (b, s, 0)), _resident((1, d)),
             _resident(w["w_in"].shape), _resident(w["gn"].shape),
             pl.BlockSpec((ts, LANES), lambda b, s: (s, 0)), pl.BlockSpec((ts, LANES), lambda b, s: (s, 0)),
             _resident(dec.shape), _resident(xi.shape), _resident(zeta.shape), _resident(gl.shape)]
    if has_init:
        args.append(state0)
        specs.append(st_blk)
    body = functools.partial(_ret_body, nb=nb, ts=ts, heads=heads, dk=dk, dv=dv, has_init=has_init)
    return pl.pallas_call(
        body, grid=(batch // nb, seq // ts), in_specs=specs,
        out_specs=[pl.BlockSpec((nb, ts, heads * dv), lambda b, s: (b, s, 0)), st_blk],
        out_shape=[jax.ShapeDtypeStruct((batch, seq, heads * dv), BF16),
                   jax.ShapeDtypeStruct((batch, heads, dk, dv), F32)],
        compiler_params=_params(2), name="retention",
    )(*args)


def _rope_tables(pos, dim, width):
    inv = 1.0 / (ROPE_THETA ** (jnp.arange(0, dim, 2, dtype=F32) / dim))
    ang = pos.astype(F32)[:, None] * inv[None, :]
    reps = width // (dim // 2)
    return jnp.tile(jnp.cos(ang), (1, reps)), jnp.tile(jnp.sin(ang), (1, reps))


def _ret_tables(pos, length, heads, dk):
    cos, sin = _rope_tables(pos, dk, dk // 2)
    lg = jnp.log(1.0 - 2.0 ** (-5.0 - jnp.arange(heads, dtype=F32)))
    idx = jnp.arange(length, dtype=F32)
    diff = idx[:, None] - idx[None, :]
    dec = jnp.where(diff >= 0, jnp.exp(lg[:, None, None] * jnp.maximum(diff, 0.0)), 0.0)
    xi = jnp.exp(lg[:, None] * (idx[None, :] + 1.0))[:, :, None]
    zeta = jnp.exp(lg[:, None] * (length - 1.0 - idx[None, :]))[:, :, None]
    gl = jnp.exp(lg * length)[:, None, None]
    return cos, sin, dec, xi, zeta, gl


def _rotate_half_cols(w):
    half = w.shape[-1] // 2
    return jnp.concatenate([-w[..., half:], w[..., :half]], axis=-1)


def _mla_weights(norm, w_dq, q_norm, w_uq, w_dkv, kv_norm, w_uk, w_uv, w_o):
    lora, heads, nope = w_uk.shape
    q_lora = w_dq.shape[1]
    rope = w_dkv.shape[1] - lora
    wq = w_uq.reshape(q_lora, heads, nope + rope)
    wq_pe = wq[:, :, nope:]
    w_uq_cat = jnp.concatenate([wq[:, :, :nope].reshape(q_lora, heads * nope),
                                wq_pe.reshape(q_lora, heads * rope),
                                _rotate_half_cols(wq_pe).reshape(q_lora, heads * rope)], axis=-1)
    wk_pe = w_dkv[:, lora:]
    wk_rot = _rotate_half_cols(wk_pe)
    w_dkv_cat = jnp.concatenate([w_dkv[:, :lora], wk_pe, wk_pe, wk_rot, wk_rot], axis=-1)
    return {
        "norm": norm[None, :], "w_dq": w_dq.astype(BF16), "q_norm": q_norm[None, :],
        "w_uq": w_uq_cat.astype(BF16), "w_dkv": w_dkv_cat.astype(BF16), "kv_norm": kv_norm[None, :],
        "w_uk": w_uk.reshape(lora, heads * nope).astype(BF16),
        "w_uv": w_uv.reshape(lora, -1).astype(BF16), "w_o": w_o.astype(BF16),
        "n_nope": heads * nope, "n_pe": heads * rope, "heads": heads,
        "scale": float(nope + rope) ** -0.5,
    }


def kernel(x_prompt, x_sample, cache_ckv, cache_kpe, state_ret, norm_ffn1, w_ffn1_gate, w_ffn1_up, w_ffn1_down, norm_mix, w_dq, q_norm, w_uq, w_dkv, kv_norm, w_uk, w_uv, w_o_mla, w_ret_in, ret_gn, w_ret_out, norm_ffn2, w_ffn2_gate, w_ffn2_up, w_ffn2_down, norm_final):
    batch, seq, d = x_prompt.shape
    streams, dseq, _ = x_sample.shape
    past = cache_ckv.shape[2]
    depth = norm_ffn1.shape[0]
    ret_heads, dk, dv = state_ret.shape[2:]
    rope = cache_kpe.shape[3]
    assert rope == LANES // 2 and w_uk.shape[3] == LANES and w_uv.shape[3] == LANES
    assert seq % ATTN_Q_TILE == 0 and seq % RET_TILE == 0 and (batch * seq) % TOKEN_TILE == 0
    assert streams % RET_SAMPLE_STREAMS == 0

    pos_p = jnp.arange(seq)
    pos_s = past + jnp.arange(dseq)
    mla_cos_p, mla_sin_p = _rope_tables(pos_p, rope, LANES)
    mla_cos_s, mla_sin_s = _rope_tables(jnp.tile(pos_s, streams), rope, LANES)
    ret_tab_p = _ret_tables(pos_p, min(RET_TILE, seq), ret_heads, dk)
    ret_tab_s = _ret_tables(pos_s, dseq, ret_heads, dk)

    xp = x_prompt.reshape(batch * seq, d)
    xs = x_sample.reshape(streams * dseq, d)
    ckv_p, kpe_p, st_p, ckv_s, kpe_s, st_s = [], [], [], [], [], []
    proj_p = proj_s = None
    for i in range(depth):
        f1 = (norm_ffn1[i][None, :], w_ffn1_gate[i].astype(BF16), w_ffn1_up[i].astype(BF16),
              w_ffn1_down[i].astype(BF16))
        xp = _ffn(xp, *f1)
        xs = _ffn(xs, *f1)
        j = i // 2
        if i % 2 == 0:
            w = _mla_weights(norm_mix[i], w_dq[j], q_norm[j], w_uq[j], w_dkv[j], kv_norm[j], w_uk[j], w_uv[j],
                             w_o_mla[j])
            heads = w["heads"]
            qn, qp, c_p, r_p, kn, v, kp2 = _mla_proj(xp, w, mla_cos_p, mla_sin_p)
            o_p = _mla_attn(qn, qp, kn, kp2, v, batch, seq, heads, w["scale"])
            qn, qp, c_s, r_s, _, _, kp2 = _mla_proj(xs, w, mla_cos_s, mla_sin_s)
            o_s = _mla_sample(qn, qp, c_s, kp2, cache_ckv, cache_kpe, j, w, streams, dseq, heads, w["scale"])
            ckv_p.append(c_p.reshape(batch, seq, -1))
            kpe_p.append(r_p.reshape(batch, seq, -1))
            ckv_s.append(c_s.reshape(streams, dseq, -1))
            kpe_s.append(r_s.reshape(streams, dseq, -1))
            w_out = w["w_o"]
        else:
            w = {"norm": norm_mix[i][None, :], "w_in": w_ret_in[j].astype(BF16),
                 "gn": ret_gn[j].reshape(ret_heads, 1, dv), "heads": ret_heads, "dk": dk, "dv": dv}
            o_p, s_p = _ret(xp.reshape(batch, seq, d), w, ret_tab_p, None, 1, min(RET_TILE, seq))
            o_s, s_s = _ret(xs.reshape(streams, dseq, d), w, ret_tab_s, state_ret[j], RET_SAMPLE_STREAMS, dseq)
            o_p = o_p.reshape(batch * seq, -1)
            o_s = o_s.reshape(streams * dseq, -1)
            st_p.append(s_p)
            st_s.append(s_s)
            w_out = w_ret_out[j].astype(BF16)
        f2 = (norm_ffn2[i][None, :], w_ffn2_gate[i].astype(BF16), w_ffn2_up[i].astype(BF16),
              w_ffn2_down[i].astype(BF16))
        final_g = norm_final[None, :] if i == depth - 1 else None
        xp = _ffn(xp, *f2, proj=(o_p, w_out), final_g=final_g)
        xs = _ffn(xs, *f2, proj=(o_s, w_out), final_g=final_g)
    return (xp.reshape(batch, seq, d), xs.reshape(streams, dseq, d),
            jnp.stack(ckv_p), jnp.stack(kpe_p), jnp.stack(st_p),
            jnp.stack(ckv_s), jnp.stack(kpe_s), jnp.stack(st_s))
```

```python
import functools

import jax
import jax.numpy as jnp
from jax import lax
from jax.experimental import pallas as pl
from jax.experimental.pallas import tpu as pltpu

F32 = jnp.float32
BF16 = jnp.bfloat16

EPS = 1e-6
ROPE_THETA = 10000.0
LOG2_E = 1.4426950408889634
CHUNK = 64
LANES = 128
V7X_VMEM_LIMIT_BYTES = 56 * 1024 * 1024

TOKEN_TILE = 512
FFN_GROUP_ROWS = 256
ATTN_Q_TILE = 256
RET_TILE = 256
RET_STEP = 512
RET_SAMPLE_STREAMS = 2


def _params(n_axes):
    return pltpu.CompilerParams(dimension_semantics=("arbitrary",) * n_axes,
                                vmem_limit_bytes=V7X_VMEM_LIMIT_BYTES)


def _resident(shape):
    zeros = (0,) * len(shape)
    return pl.BlockSpec(shape, lambda *_: zeros, pipeline_mode=pl.Buffered(1))


def _layer_resident(arr, layer):
    idx = (layer,) + (0,) * (arr.ndim - 1)
    return pl.BlockSpec((None,) + arr.shape[1:], lambda *_: idx, pipeline_mode=pl.Buffered(1))


def _dot(a, b):
    return jnp.dot(a, b, preferred_element_type=F32)


def _dot_nt(a, b):
    return lax.dot_general(a, b, (((1,), (1,)), ((), ())), preferred_element_type=F32)


def _dot_tn(a, b):
    return lax.dot_general(a, b, (((0,), (0,)), ((), ())), preferred_element_type=F32)


def _rms(x, g):
    ms = jnp.mean(x * x, axis=-1, keepdims=True)
    return x * lax.rsqrt(ms + EPS) * g


def _silu(x):
    return x * (1.0 / (1.0 + jnp.exp(-x)))


def _ffn_body(*refs, has_proj, has_final):
    it = iter(refs)
    x_ref = next(it)
    if has_proj:
        o_ref, wp_ref = next(it), next(it)
    g_ref, wg_ref, wu_ref, wd_ref = next(it), next(it), next(it), next(it)
    if has_final:
        gf_ref = next(it)
    out_ref = next(it)

    sub = min(FFN_GROUP_ROWS, x_ref.shape[0])
    for si in range(x_ref.shape[0] // sub):
        rows = slice(si * sub, (si + 1) * sub)
        x = x_ref[rows, :]
        if has_proj:
            x = x + _dot(o_ref[rows, :], wp_ref[...])
        h = _rms(x, g_ref[...]).astype(BF16)
        gate = _dot(h, wg_ref[...])
        up = _dot(h, wu_ref[...])
        a = (_silu(gate) * up).astype(BF16)
        r = x + 0.5 * _dot(a, wd_ref[...])
        if has_final:
            r = _rms(r, gf_ref[...])
        out_ref[rows, :] = r


def _ffn(x, layer, g, wg, wu, wd, proj=None, final_g=None):
    t, d = x.shape
    tm = min(TOKEN_TILE, t)
    row = lambda i: (i, 0)
    args, specs = [x], [pl.BlockSpec((tm, d), row)]
    if proj is not None:
        o, wp, wp_layer = proj
        args += [o, wp]
        specs += [pl.BlockSpec((tm, o.shape[1]), row), _layer_resident(wp, wp_layer)]
    args += [g, wg, wu, wd]
    specs += [_layer_resident(a, layer) for a in (g, wg, wu, wd)]
    if final_g is not None:
        args.append(final_g)
        specs.append(_resident(final_g.shape))
    body = functools.partial(_ffn_body, has_proj=proj is not None, has_final=final_g is not None)
    return pl.pallas_call(
        body, grid=(t // tm,), in_specs=specs,
        out_specs=pl.BlockSpec((tm, d), row),
        out_shape=jax.ShapeDtypeStruct((t, d), F32),
        compiler_params=_params(1), name="ffn_half",
    )(*args)


def _mla_proj_body(*refs, n_nope, n_pe, lora, has_prev):
    (x_ref, g_ref, wdq_ref, qn_ref, wuq_ref, wdkv_ref, kvn_ref, wukt_ref, wuv_ref, cos_ref, sin_ref) = refs[:11]
    qnope_ref, qpe_ref, ckv_ref, kpe_ref, knt_ref, v_ref, kpt_ref = refs[11 + 2 * has_prev:]
    h = _rms(x_ref[...], g_ref[...]).astype(BF16)
    cq = _rms(_dot(h, wdq_ref[...]), qn_ref[...]).astype(BF16)
    q = _dot(cq, wuq_ref[...])
    cos, sin = cos_ref[...], sin_ref[...]
    reps = n_pe // LANES
    cos_q = jnp.concatenate([cos] * reps, axis=-1)
    sin_q = jnp.concatenate([sin] * reps, axis=-1)
    qnope_ref[...] = q[:, :n_nope].astype(BF16)
    qpe_ref[...] = (q[:, n_nope:n_nope + n_pe] * cos_q + q[:, n_nope + n_pe:] * sin_q).astype(BF16)
    kv = _dot(h, wdkv_ref[...])
    ckv = _rms(kv[:, :lora], kvn_ref[...])
    ckv_ref[...] = ckv
    kpe2 = kv[:, lora:lora + LANES] * cos + kv[:, lora + LANES:] * sin
    kpe_ref[...] = kpe2[:, :LANES // 2]
    kpt_ref[...] = kpe2.T.astype(BF16)
    cb = ckv.astype(BF16)
    knt_ref[...] = _dot_nt(wukt_ref[...], cb).astype(BF16)
    v_ref[...] = _dot(cb, wuv_ref[...]).astype(BF16)


def _mla_proj(x, layer, w, cos, sin, prev):
    t, d = x.shape
    n_layers = w["w_dq"].shape[0]
    tm = min(TOKEN_TILE, t)
    n_pos_tiles = cos.shape[0] // tm
    n_nope, n_pe = w["n_nope"], w["n_pe"]
    lora, rope = w["kv_norm"].shape[2], LANES // 2
    n_v = w["w_uv"].shape[2]
    row = lambda i: (i, 0)
    col = lambda i: (0, i)
    lrow = lambda i: (layer, i, 0)
    pos = lambda i: (i % n_pos_tiles, 0)
    names = ("norm", "w_dq", "q_norm", "w_uq", "w_dkv", "kv_norm", "w_ukt", "w_uv")
    args = [x] + [w[n] for n in names] + [cos, sin]
    specs = ([pl.BlockSpec((tm, d), row)] + [_layer_resident(w[n], layer) for n in names]
             + [pl.BlockSpec((tm, LANES), pos), pl.BlockSpec((tm, LANES), pos)])
    aliases = {}
    if prev is not None:
        aliases = {len(args): 2, len(args) + 1: 3}
        args += list(prev)
        specs += [pl.BlockSpec(memory_space=pl.ANY)] * 2
    body = functools.partial(_mla_proj_body, n_nope=n_nope, n_pe=n_pe, lora=lora, has_prev=prev is not None)
    return pl.pallas_call(
        body, grid=(t // tm,), in_specs=specs,
        out_specs=[pl.BlockSpec((tm, n_nope), row), pl.BlockSpec((tm, n_pe), row),
                   pl.BlockSpec((None, tm, lora), lrow), pl.BlockSpec((None, tm, rope), lrow),
                   pl.BlockSpec((n_nope, tm), col), pl.BlockSpec((tm, n_v), row),
                   pl.BlockSpec((LANES, tm), col)],
        out_shape=[jax.ShapeDtypeStruct((t, n_nope), BF16), jax.ShapeDtypeStruct((t, n_pe), BF16),
                   jax.ShapeDtypeStruct((n_layers, t, lora), F32), jax.ShapeDtypeStruct((n_layers, t, rope), F32),
                   jax.ShapeDtypeStruct((n_nope, t), BF16), jax.ShapeDtypeStruct((t, n_v), BF16),
                   jax.ShapeDtypeStruct((LANES, t), BF16)],
        input_output_aliases=aliases,
        compiler_params=_params(1), name="mla_proj",
    )(*args)


def _mla_attn_body(qn_ref, qp_ref, knt_ref, kpt_ref, v_ref, o_ref, kcat_ref, vext_ref, *, seq, tq, exp2_scale):
    head = pl.program_id(1)
    lane = lax.broadcasted_iota(jnp.int32, (tq, LANES), 1)
    keep = (lane >= LANES // 2).astype(jnp.int32) == head % 2
    kcat_ref[:LANES, :] = knt_ref[...]
    kcat_ref[LANES:, :] = kpt_ref[...]
    vext_ref[:, :LANES] = v_ref[...]
    vext_ref[:, LANES:] = jnp.ones((seq, LANES), BF16)
    qc = lax.broadcasted_iota(jnp.int32, (tq, tq), 0) // CHUNK
    kc = lax.broadcasted_iota(jnp.int32, (tq, tq), 1) // CHUNK
    visible = kc <= qc
    for qi in reversed(range(seq // tq)):
        q0 = qi * tq
        qp = jnp.where(keep, qp_ref[q0:q0 + tq, :].astype(F32), 0.0).astype(BF16)
        qh = jnp.concatenate([qn_ref[q0:q0 + tq, :], qp], axis=-1)
        s_d = jnp.where(visible, _dot(qh, kcat_ref[:, q0:q0 + tq]), -jnp.inf)
        m = jnp.max(s_d, axis=-1, keepdims=True)
        if qi > 0:
            s_p = _dot(qh, kcat_ref[:, 0:q0])
            m = jnp.maximum(m, jnp.max(s_p, axis=-1, keepdims=True))
        acc = _dot(jnp.exp2((s_d - m) * exp2_scale).astype(BF16), vext_ref[q0:q0 + tq, :])
        if qi > 0:
            acc = acc + _dot(jnp.exp2((s_p - m) * exp2_scale).astype(BF16), vext_ref[0:q0, :])
        o_ref[q0:q0 + tq, :] = (acc[:, :LANES] * (1.0 / acc[:, LANES:])).astype(BF16)


def _mla_attn(qnope, qpe, knt, kpt, v, batch, seq, heads, scale):
    t = batch * seq
    tq = min(ATTN_Q_TILE, seq)
    body = functools.partial(_mla_attn_body, seq=seq, tq=tq, exp2_scale=scale * LOG2_E)
    blk, blk_t = (seq, LANES), (LANES, seq)
    return pl.pallas_call(
        body, grid=(batch, heads),
        in_specs=[pl.BlockSpec(blk, lambda b, h: (b, h)), pl.BlockSpec(blk, lambda b, h: (b, h // 2)),
                  pl.BlockSpec(blk_t, lambda b, h: (h, b)), pl.BlockSpec(blk_t, lambda b, h: (0, b)),
                  pl.BlockSpec(blk, lambda b, h: (b, h))],
        out_specs=pl.BlockSpec(blk, lambda b, h: (b, h)),
        out_shape=jax.ShapeDtypeStruct((t, heads * LANES), BF16),
        scratch_shapes=[pltpu.VMEM((2 * LANES, seq), BF16), pltpu.VMEM((seq, 2 * LANES), BF16)],
        compiler_params=_params(2), name="mla_attn",
    )(qnope, qpe, knt, kpt, v)


def _mla_sample_body(qn_ref, qp_ref, ckvn_ref, kpen_ref, cc_ref, ck_ref, wukt_ref, wuv_ref, o_ref,
                     *, heads, dq, scale):
    qn, qp = qn_ref[...], qp_ref[...]
    wukt, wuv = wukt_ref[...], wuv_ref[...]
    rope = LANES // 2
    qlat = jnp.concatenate(
        [_dot(qn[:, LANES * h:LANES * (h + 1)], wukt[LANES * h:LANES * (h + 1), :]) for h in range(heads)],
        axis=0).astype(BF16)
    qpe = jnp.concatenate([qp[:, rope * h:rope * (h + 1)] for h in range(heads)], axis=0)
    cc = cc_ref[...].astype(BF16)
    ck = ck_ref[...].astype(BF16)
    cn = ckvn_ref[...].astype(BF16)
    kn = kpen_ref[...].astype(BF16)
    s_past = (_dot_nt(qlat, cc) + _dot_nt(qpe, ck)) * scale
    s_new = (_dot_nt(qlat, cn) + _dot_nt(qpe, kn)) * scale
    m = jnp.maximum(jnp.max(s_past, axis=-1, keepdims=True), jnp.max(s_new, axis=-1, keepdims=True))
    p_past = jnp.exp(s_past - m)
    p_new = jnp.exp(s_new - m)
    l = jnp.sum(p_past, axis=-1, keepdims=True) + jnp.sum(p_new, axis=-1, keepdims=True)
    o_lat = _dot(p_past.astype(BF16), cc) + _dot(p_new.astype(BF16), cn)
    ob = (o_lat * (1.0 / l)).astype(BF16)
    o = jnp.concatenate(
        [_dot(ob[dq * h:dq * (h + 1), :], wuv[:, LANES * h:LANES * (h + 1)]) for h in range(heads)], axis=-1)
    o_ref[...] = o.astype(BF16)


def _mla_sample(qnope, qpe, ckv_new, kpe_new, cache_ckv, cache_kpe, layer, w, streams, dq, heads, scale):
    past, lora = cache_ckv.shape[2], cache_ckv.shape[3]
    rope = cache_kpe.shape[3]
    n_v = w["w_uv"].shape[2]
    row = lambda b: (b, 0)
    lrow = lambda b: (layer, b, 0)
    body = functools.partial(_mla_sample_body, heads=heads, dq=dq, scale=scale)
    return pl.pallas_call(
        body, grid=(streams,),
        in_specs=[pl.BlockSpec((dq, qnope.shape[1]), row), pl.BlockSpec((dq, qpe.shape[1]), row),
                  pl.BlockSpec((None, dq, lora), lrow), pl.BlockSpec((None, dq, rope), lrow),
                  pl.BlockSpec((None, None, past, lora), lambda b: (layer, b, 0, 0)),
                  pl.BlockSpec((None, None, past, rope), lambda b: (layer, b, 0, 0)),
                  _layer_resident(w["w_ukt"], layer), _layer_resident(w["w_uv"], layer)],
        out_specs=pl.BlockSpec((dq, n_v), row),
        out_shape=jax.ShapeDtypeStruct((streams * dq, n_v), BF16),
        compiler_params=_params(1), name="mla_sample",
    )(qnope, qpe, ckv_new, kpe_new, cache_ckv, cache_kpe, w["w_ukt"], w["w_uv"])


def _ret_body(*refs, nb, ts, chunk, heads, dk, dv, has_init, has_prev):
    it = iter(refs)
    x_ref, g_ref, win_ref, gn_ref = next(it), next(it), next(it), next(it)
    cos_ref, sin_ref, dec_ref, xi_ref, zeta_ref, gl_ref = (next(it), next(it), next(it), next(it), next(it),
                                                           next(it))
    if has_init:
        s0_ref = next(it)
    if has_prev:
        next(it)
    y_ref, st_ref = next(it), next(it)

    @pl.when(pl.program_id(1) == 0)
    def _():
        st_ref[...] = s0_ref[...] if has_init else jnp.zeros_like(st_ref)

    d = x_ref.shape[-1]
    half = dk // 2
    k_off, v_off, g_off = heads * dk, 2 * heads * dk, 2 * heads * dk + heads * dv

    def project(x):
        return _dot(_rms(x, g_ref[...]).astype(BF16), win_ref[...])

    shared = chunk < LANES
    if shared:
        qkvg_all = project(x_ref[...].reshape(nb * ts, d))
    for bi in range(nb):
        for r0 in range(0, ts, chunk):
            if shared:
                qkvg = qkvg_all[bi * ts + r0:bi * ts + r0 + chunk, :]
            else:
                qkvg = project(x_ref[bi, r0:r0 + chunk, :])
            cos, sin = cos_ref[r0:r0 + chunk, :], sin_ref[r0:r0 + chunk, :]

            def rot(u):
                u1, u2 = u[:, :half], u[:, half:]
                return jnp.concatenate([u1 * cos - u2 * sin, u1 * sin + u2 * cos], axis=-1)

            ys = []
            for hh in range(heads):
                q = rot(qkvg[:, dk * hh:dk * (hh + 1)])
                k = rot(qkvg[:, k_off + dk * hh:k_off + dk * (hh + 1)]) * (dk ** -0.5)
                v = qkvg[:, v_off + dv * hh:v_off + dv * (hh + 1)].astype(BF16)
                gate = qkvg[:, g_off + dv * hh:g_off + dv * (hh + 1)]
                inner = (_dot_nt(q.astype(BF16), k.astype(BF16)) * dec_ref[hh]).astype(BF16)
                state = st_ref[bi, hh]
                o = _dot(inner, v) + _dot((q * xi_ref[hh]).astype(BF16), state.astype(BF16))
                st_ref[bi, hh] = gl_ref[hh] * state + _dot_tn((k * zeta_ref[hh]).astype(BF16), v)
                ys.append((_silu(gate) * _rms(o, gn_ref[hh])).astype(BF16))
            y_ref[bi, r0:r0 + chunk, :] = jnp.concatenate(ys, axis=-1)


def _ret(x, layer, mix_layer, w, tables, state0, prev, nb, ts):
    chunk = tables[2].shape[1]
    batch, seq, d = x.shape
    heads, dk, dv = w["heads"], w["dk"], w["dv"]
    n_layers = w["w_in"].shape[0]
    cos, sin, dec, xi, zeta, gl = tables
    st_blk = pl.BlockSpec((None, nb, heads, dk, dv), lambda b, s: (layer, b, 0, 0, 0))
    args = [x, w["norm"], w["w_in"], w["gn"], cos, sin, dec, xi, zeta, gl]
    specs = [pl.BlockSpec((nb, ts, d), lambda b, s: (b, s, 0)), _layer_resident(w["norm"], mix_layer),
             _layer_resident(w["w_in"], layer), _layer_resident(w["gn"], layer),
             pl.BlockSpec((ts, LANES), lambda b, s: (s, 0)), pl.BlockSpec((ts, LANES), lambda b, s: (s, 0)),
             _resident(dec.shape), _resident(xi.shape), _resident(zeta.shape), _resident(gl.shape)]
    if state0 is not None:
        args.append(state0)
        specs.append(st_blk)
    aliases = {}
    if prev is not None:
        aliases = {len(args): 1}
        args.append(prev)
        specs.append(pl.BlockSpec(memory_space=pl.ANY))
    body = functools.partial(_ret_body, nb=nb, ts=ts, chunk=chunk, heads=heads, dk=dk, dv=dv,
                             has_init=state0 is not None, has_prev=prev is not None)
    return pl.pallas_call(
        body, grid=(batch // nb, seq // ts), in_specs=specs,
        out_specs=[pl.BlockSpec((nb, ts, heads * dv), lambda b, s: (b, s, 0)), st_blk],
        out_shape=[jax.ShapeDtypeStruct((batch, seq, heads * dv), BF16),
                   jax.ShapeDtypeStruct((n_layers, batch, heads, dk, dv), F32)],
        input_output_aliases=aliases,
        compiler_params=_params(2), name="retention",
    )(*args)


def _rope_tables(pos, dim, width):
    inv = 1.0 / (ROPE_THETA ** (jnp.arange(0, dim, 2, dtype=F32) / dim))
    ang = pos.astype(F32)[:, None] * inv[None, :]
    reps = width // (dim // 2)
    return jnp.tile(jnp.cos(ang), (1, reps)), jnp.tile(jnp.sin(ang), (1, reps))


def _ret_tables(pos, length, heads, dk):
    cos, sin = _rope_tables(pos, dk, dk // 2)
    lg = jnp.log(1.0 - 2.0 ** (-5.0 - jnp.arange(heads, dtype=F32)))
    idx = jnp.arange(length, dtype=F32)
    diff = idx[:, None] - idx[None, :]
    dec = jnp.where(diff >= 0, jnp.exp(lg[:, None, None] * jnp.maximum(diff, 0.0)), 0.0)
    xi = jnp.exp(lg[:, None] * (idx[None, :] + 1.0))[:, :, None]
    zeta = jnp.exp(lg[:, None] * (length - 1.0 - idx[None, :]))[:, :, None]
    gl = jnp.exp(lg * length)[:, None, None]
    return cos, sin, dec, xi, zeta, gl


def _rotate_half_cols(w):
    half = w.shape[-1] // 2
    return jnp.concatenate([-w[..., half:], w[..., :half]], axis=-1)


def _mla_weights(norm, w_dq, q_norm, w_uq, w_dkv, kv_norm, w_uk, w_uv, w_o):
    n, lora, heads, nope = w_uk.shape
    q_lora = w_dq.shape[2]
    rope = w_dkv.shape[2] - lora
    wq = w_uq.reshape(n, q_lora, heads, nope + rope)
    wq_pe = wq[..., nope:]
    w_uq_cat = jnp.concatenate([wq[..., :nope].reshape(n, q_lora, heads * nope),
                                wq_pe.reshape(n, q_lora, heads * rope),
                                _rotate_half_cols(wq_pe).reshape(n, q_lora, heads * rope)], axis=-1)
    wk_pe = w_dkv[..., lora:]
    wk_rot = _rotate_half_cols(wk_pe)
    w_dkv_cat = jnp.concatenate([w_dkv[..., :lora], wk_pe, wk_pe, wk_rot, wk_rot], axis=-1)
    return {
        "norm": norm[:, None, :], "w_dq": w_dq.astype(BF16), "q_norm": q_norm[:, None, :],
        "w_uq": w_uq_cat.astype(BF16), "w_dkv": w_dkv_cat.astype(BF16), "kv_norm": kv_norm[:, None, :],
        "w_ukt": jnp.swapaxes(w_uk.reshape(n, lora, heads * nope), 1, 2).astype(BF16),
        "w_uv": w_uv.reshape(n, lora, -1).astype(BF16), "w_o": w_o.astype(BF16),
        "n_nope": heads * nope, "n_pe": heads * rope, "heads": heads,
        "scale": float(nope + rope) ** -0.5,
    }


def kernel(x_prompt, x_sample, cache_ckv, cache_kpe, state_ret, norm_ffn1, w_ffn1_gate, w_ffn1_up, w_ffn1_down, norm_mix, w_dq, q_norm, w_uq, w_dkv, kv_norm, w_uk, w_uv, w_o_mla, w_ret_in, ret_gn, w_ret_out, norm_ffn2, w_ffn2_gate, w_ffn2_up, w_ffn2_down, norm_final):
    batch, seq, d = x_prompt.shape
    streams, dseq, _ = x_sample.shape
    past = cache_ckv.shape[2]
    depth = norm_ffn1.shape[0]
    n_ret, _, ret_heads, dk, dv = state_ret.shape
    rope = cache_kpe.shape[3]
    assert rope == LANES // 2 and w_uk.shape[3] == LANES and w_uv.shape[3] == LANES
    assert seq % ATTN_Q_TILE == 0 and seq % RET_STEP == 0 and RET_STEP % RET_TILE == 0 and seq % TOKEN_TILE == 0
    assert streams % RET_SAMPLE_STREAMS == 0

    pos_p = jnp.arange(seq)
    pos_s = past + jnp.arange(dseq)
    mla_cos_p, mla_sin_p = _rope_tables(pos_p, rope, LANES)
    mla_cos_s, mla_sin_s = _rope_tables(jnp.tile(pos_s, streams), rope, LANES)
    ret_tab_p = _ret_tables(pos_p, RET_TILE, ret_heads, dk)
    ret_tab_s = _ret_tables(pos_s, dseq, ret_heads, dk)

    ffn1 = (norm_ffn1[:, None, :], w_ffn1_gate.astype(BF16), w_ffn1_up.astype(BF16), w_ffn1_down.astype(BF16))
    ffn2 = (norm_ffn2[:, None, :], w_ffn2_gate.astype(BF16), w_ffn2_up.astype(BF16), w_ffn2_down.astype(BF16))
    mla = _mla_weights(norm_mix[0::2], w_dq, q_norm, w_uq, w_dkv, kv_norm, w_uk, w_uv, w_o_mla)
    ret = {"norm": norm_mix[:, None, :], "w_in": w_ret_in.astype(BF16),
           "gn": ret_gn.reshape(n_ret, ret_heads, 1, dv), "heads": ret_heads, "dk": dk, "dv": dv}
    w_ret_out_b = w_ret_out.astype(BF16)
    final_g = norm_final[None, :]

    xp = x_prompt.reshape(batch * seq, d)
    xs = x_sample.reshape(streams * dseq, d)
    lat_p = lat_s = st_p = st_s = None
    for i in range(depth):
        xp = _ffn(xp, i, *ffn1)
        xs = _ffn(xs, i, *ffn1)
        j = i // 2
        if i % 2 == 0:
            heads, scale = mla["heads"], mla["scale"]
            qn, qp, c_p, r_p, knt, v, kpt = _mla_proj(xp, j, mla, mla_cos_p, mla_sin_p, lat_p)
            lat_p = (c_p, r_p)
            o_p = _mla_attn(qn, qp, knt, kpt, v, batch, seq, heads, scale)
            qn, qp, c_s, r_s, _, _, _ = _mla_proj(xs, j, mla, mla_cos_s, mla_sin_s, lat_s)
            lat_s = (c_s, r_s)
            o_s = _mla_sample(qn, qp, c_s, r_s, cache_ckv, cache_kpe, j, mla, streams, dseq, heads, scale)
            out_proj = (mla["w_o"], j)
        else:
            o_p, st_p = _ret(xp.reshape(batch, seq, d), j, i, ret, ret_tab_p, None, st_p, 1, RET_STEP)
            o_s, st_s = _ret(xs.reshape(streams, dseq, d), j, i, ret, ret_tab_s, state_ret, st_s,
                             RET_SAMPLE_STREAMS, dseq)
            o_p = o_p.reshape(batch * seq, -1)
            o_s = o_s.reshape(streams * dseq, -1)
            out_proj = (w_ret_out_b, j)
        fg = final_g if i == depth - 1 else None
        xp = _ffn(xp, i, *ffn2, proj=(o_p,) + out_proj, final_g=fg)
        xs = _ffn(xs, i, *ffn2, proj=(o_s,) + out_proj, final_g=fg)
    n_mla = lat_p[0].shape[0]
    return (xp.reshape(batch, seq, d), xs.reshape(streams, dseq, d),
            lat_p[0].reshape(n_mla, batch, seq, -1), lat_p[1].reshape(n_mla, batch, seq, -1), st_p,
            lat_s[0].reshape(n_mla, streams, dseq, -1), lat_s[1].reshape(n_mla, streams, dseq, -1), st_s)
```

```python
import functools

import jax
import jax.numpy as jnp
from jax import lax
from jax.experimental import pallas as pl
from jax.experimental.pallas import tpu as pltpu

F32 = jnp.float32
BF16 = jnp.bfloat16

EPS = 1e-6
ROPE_THETA = 10000.0
LOG2_E = 1.4426950408889634
CHUNK = 64
LANES = 128
V7X_VMEM_LIMIT_BYTES = 56 * 1024 * 1024

TOKEN_TILE = 512
FFN_GROUP_ROWS = 256
ATTN_Q_TILE = 256
RET_TILE = 256
RET_STEP = 512
RET_SAMPLE_STREAMS = 2


def _params(n_axes):
    return pltpu.CompilerParams(dimension_semantics=("arbitrary",) * n_axes,
                                vmem_limit_bytes=V7X_VMEM_LIMIT_BYTES)


def _resident(shape):
    zeros = (0,) * len(shape)
    return pl.BlockSpec(shape, lambda *_: zeros, pipeline_mode=pl.Buffered(1))


def _layer_resident(arr, layer):
    idx = (layer,) + (0,) * (arr.ndim - 1)
    return pl.BlockSpec((None,) + arr.shape[1:], lambda *_: idx, pipeline_mode=pl.Buffered(1))


def _dot(a, b):
    return jnp.dot(a, b, preferred_element_type=F32)


def _dot_nt(a, b):
    return lax.dot_general(a, b, (((1,), (1,)), ((), ())), preferred_element_type=F32)


def _dot_tn(a, b):
    return lax.dot_general(a, b, (((0,), (0,)), ((), ())), preferred_element_type=F32)


def _rms(x, g):
    ms = jnp.mean(x * x, axis=-1, keepdims=True)
    return x * lax.rsqrt(ms + EPS) * g


def _silu(x):
    return x * (1.0 / (1.0 + jnp.exp(-x)))


def _ffn_body(*refs, has_proj, has_final, mla):
    it = iter(refs)
    x_ref = next(it)
    if has_proj:
        o_ref, wp_ref = next(it), next(it)
    g_ref, wg_ref, wu_ref, wd_ref = next(it), next(it), next(it), next(it)
    if has_final:
        gf_ref = next(it)
    if mla is not None:
        mla_w = [next(it) for _ in MLA_WEIGHT_NAMES]
        cos_ref, sin_ref = next(it), next(it)
        if mla["has_prev"]:
            next(it), next(it)
    out_ref = next(it)
    mla_out = [next(it) for _ in range(N_MLA_OUTPUTS)] if mla is not None else None

    sub = min(FFN_GROUP_ROWS, x_ref.shape[0])
    if has_proj:
        mixed = _dot(o_ref[...], wp_ref[...])
    results = []
    for si in range(x_ref.shape[0] // sub):
        rows = slice(si * sub, (si + 1) * sub)
        x = x_ref[rows, :]
        if has_proj:
            x = x + mixed[rows, :]
        h = _rms(x, g_ref[...]).astype(BF16)
        gate = _dot(h, wg_ref[...])
        up = _dot(h, wu_ref[...])
        a = (_silu(gate) * up).astype(BF16)
        r = x + 0.5 * _dot(a, wd_ref[...])
        if has_final:
            r = _rms(r, gf_ref[...])
        out_ref[rows, :] = r
        results.append(r)
    if mla is not None:
        tile = results[0] if len(results) == 1 else jnp.concatenate(results, axis=0)
        _mla_project(tile, mla_w, cos_ref, sin_ref, mla_out, n_nope=mla["n_nope"], n_pe=mla["n_pe"],
                     lora=mla["lora"], kpe_by_feature=mla["kpe_by_feature"])


def _ffn(x, layer, g, wg, wu, wd, proj=None, final_g=None, mla=None):
    t, d = x.shape
    tm = min(TOKEN_TILE, t)
    row = lambda i: (i, 0)
    args, specs = [x], [pl.BlockSpec((tm, d), row)]
    if proj is not None:
        o, wp, wp_layer = proj
        args += [o, wp]
        specs += [pl.BlockSpec((tm, o.shape[1]), row), _layer_resident(wp, wp_layer)]
    args += [g, wg, wu, wd]
    specs += [_layer_resident(a, layer) for a in (g, wg, wu, wd)]
    if final_g is not None:
        args.append(final_g)
        specs.append(_resident(final_g.shape))
    out_specs = [pl.BlockSpec((tm, d), row)]
    out_shape = [jax.ShapeDtypeStruct((t, d), F32)]
    aliases, mla_static = {}, None
    if mla is not None:
        m_layer, w, cos, sin, prev, kpe_seq = mla
        n_layers = w["w_dq"].shape[0]
        n_pos_tiles = cos.shape[0] // tm
        n_nope, n_pe = w["n_nope"], w["n_pe"]
        lora, rope = w["kv_norm"].shape[2], LANES // 2
        n_v = w["w_uv"].shape[2]
        col = lambda i: (0, i)
        lrow = lambda i: (m_layer, i, 0)
        pos = lambda i: (i % n_pos_tiles, 0)
        args += [w[n] for n in MLA_WEIGHT_NAMES] + [cos, sin]
        specs += ([_layer_resident(w[n], m_layer) for n in MLA_WEIGHT_NAMES]
                  + [pl.BlockSpec((tm, LANES), pos), pl.BlockSpec((tm, LANES), pos)])
        if prev is not None:
            aliases = {len(args): 3, len(args) + 1: 4}
            args += list(prev)
            specs += [pl.BlockSpec(memory_space=pl.ANY)] * 2
        if kpe_seq is not None:
            per_seq = kpe_seq // tm
            kpe_spec = pl.BlockSpec((None, None, rope, tm), lambda i: (m_layer, i // per_seq, 0, i % per_seq))
            kpe_shape = jax.ShapeDtypeStruct((n_layers, t // kpe_seq, rope, kpe_seq), F32)
        else:
            kpe_spec = pl.BlockSpec((None, tm, rope), lrow)
            kpe_shape = jax.ShapeDtypeStruct((n_layers, t, rope), F32)
        out_specs += [pl.BlockSpec((tm, n_nope), row), pl.BlockSpec((tm, n_pe), row),
                      pl.BlockSpec((None, tm, lora), lrow), kpe_spec,
                      pl.BlockSpec((n_nope, tm), col), pl.BlockSpec((tm, n_v), row),
                      pl.BlockSpec((LANES, tm), col)]
        out_shape += [jax.ShapeDtypeStruct((t, n_nope), BF16), jax.ShapeDtypeStruct((t, n_pe), BF16),
                      jax.ShapeDtypeStruct((n_layers, t, lora), F32), kpe_shape,
                      jax.ShapeDtypeStruct((n_nope, t), BF16), jax.ShapeDtypeStruct((t, n_v), BF16),
                      jax.ShapeDtypeStruct((LANES, t), BF16)]
        mla_static = {"n_nope": n_nope, "n_pe": n_pe, "lora": lora, "has_prev": prev is not None,
                      "kpe_by_feature": kpe_seq is not None}
    body = functools.partial(_ffn_body, has_proj=proj is not None, has_final=final_g is not None, mla=mla_static)
    outs = pl.pallas_call(
        body, grid=(t // tm,), in_specs=specs, out_specs=out_specs, out_shape=out_shape,
        input_output_aliases=aliases,
        compiler_params=_params(1), name="ffn_half",
    )(*args)
    return outs[0] if mla is None else outs


MLA_WEIGHT_NAMES = ("norm", "w_dq", "q_norm", "w_uq", "w_dkv", "kv_norm", "w_ukt", "w_uv")
N_MLA_OUTPUTS = 7


def _mla_project(x, w_refs, cos_ref, sin_ref, out_refs, *, n_nope, n_pe, lora, kpe_by_feature):
    g_ref, wdq_ref, qn_ref, wuq_ref, wdkv_ref, kvn_ref, wukt_ref, wuv_ref = w_refs
    qnope_ref, qpe_ref, ckv_ref, kpe_ref, knt_ref, v_ref, kpt_ref = out_refs
    reps = n_pe // LANES
    rope = LANES // 2
    h = _rms(x, g_ref[...]).astype(BF16)
    cq = _rms(_dot(h, wdq_ref[...]), qn_ref[...]).astype(BF16)
    q = _dot(cq, wuq_ref[...])
    cos, sin = cos_ref[...], sin_ref[...]
    cos_q = jnp.concatenate([cos] * reps, axis=-1)
    sin_q = jnp.concatenate([sin] * reps, axis=-1)
    qnope_ref[...] = q[:, :n_nope].astype(BF16)
    qpe_ref[...] = (q[:, n_nope:n_nope + n_pe] * cos_q + q[:, n_nope + n_pe:] * sin_q).astype(BF16)
    kv = _dot(h, wdkv_ref[...])
    ckv = _rms(kv[:, :lora], kvn_ref[...])
    ckv_ref[...] = ckv
    kpe2 = kv[:, lora:lora + LANES] * cos + kv[:, lora + LANES:] * sin
    kpe2_t = kpe2.T
    if kpe_by_feature:
        kpe_ref[...] = kpe2_t[:rope, :]
    else:
        kpe_ref[...] = kpe2[:, :rope]
    kpt_ref[...] = kpe2_t.astype(BF16)
    cb = ckv.astype(BF16)
    knt_ref[...] = _dot_nt(wukt_ref[...], cb).astype(BF16)
    v_ref[...] = _dot(cb, wuv_ref[...]).astype(BF16)


def _mla_attn_body(qn_ref, qp_ref, knt_ref, kpt_ref, v_ref, o_ref, kcat_ref, vext_ref, *, seq, tq, exp2_scale):
    head = pl.program_id(1)
    lane = lax.broadcasted_iota(jnp.int32, (tq, LANES), 1)
    keep = (lane >= LANES // 2).astype(jnp.int32) == head % 2
    kcat_ref[:LANES, :] = knt_ref[...]
    kcat_ref[LANES:, :] = kpt_ref[...]
    vext_ref[:, :LANES] = v_ref[...]
    vext_ref[:, LANES:] = jnp.ones((seq, LANES), BF16)
    qc = lax.broadcasted_iota(jnp.int32, (tq, tq), 0) // CHUNK
    kc = lax.broadcasted_iota(jnp.int32, (tq, tq), 1) // CHUNK
    visible = kc <= qc
    for qi in reversed(range(seq // tq)):
        q0 = qi * tq
        qp = jnp.where(keep, qp_ref[q0:q0 + tq, :].astype(F32), 0.0).astype(BF16)
        qh = jnp.concatenate([qn_ref[q0:q0 + tq, :], qp], axis=-1)
        s_d = jnp.where(visible, _dot(qh, kcat_ref[:, q0:q0 + tq]), -jnp.inf)
        m = jnp.max(s_d, axis=-1, keepdims=True)
        if qi > 0:
            s_p = _dot(qh, kcat_ref[:, 0:q0])
            m = jnp.maximum(m, jnp.max(s_p, axis=-1, keepdims=True))
        acc = _dot(jnp.exp2((s_d - m) * exp2_scale).astype(BF16), vext_ref[q0:q0 + tq, :])
        if qi > 0:
            acc = acc + _dot(jnp.exp2((s_p - m) * exp2_scale).astype(BF16), vext_ref[0:q0, :])
        o_ref[q0:q0 + tq, :] = (acc[:, :LANES] * (1.0 / acc[:, LANES:])).astype(BF16)


def _mla_attn(qnope, qpe, knt, kpt, v, batch, seq, heads, scale):
    t = batch * seq
    tq = min(ATTN_Q_TILE, seq)
    body = functools.partial(_mla_attn_body, seq=seq, tq=tq, exp2_scale=scale * LOG2_E)
    blk, blk_t = (seq, LANES), (LANES, seq)
    return pl.pallas_call(
        body, grid=(batch, heads),
        in_specs=[pl.BlockSpec(blk, lambda b, h: (b, h)), pl.BlockSpec(blk, lambda b, h: (b, h // 2)),
                  pl.BlockSpec(blk_t, lambda b, h: (h, b)), pl.BlockSpec(blk_t, lambda b, h: (0, b)),
                  pl.BlockSpec(blk, lambda b, h: (b, h))],
        out_specs=pl.BlockSpec(blk, lambda b, h: (b, h)),
        out_shape=jax.ShapeDtypeStruct((t, heads * LANES), BF16),
        scratch_shapes=[pltpu.VMEM((2 * LANES, seq), BF16), pltpu.VMEM((seq, 2 * LANES), BF16)],
        compiler_params=_params(2), name="mla_attn",
    )(qnope, qpe, knt, kpt, v)


def _mla_sample_body(qn_ref, qp_ref, ckvn_ref, kpen_ref, cc_ref, ckt_ref, wukt_ref, wuv_ref, o_ref,
                     *, heads, dq, scale):
    qn, qp = qn_ref[...], qp_ref[...]
    wukt, wuv = wukt_ref[...], wuv_ref[...]
    rope = LANES // 2
    qlat = jnp.concatenate(
        [_dot(qn[:, LANES * h:LANES * (h + 1)], wukt[LANES * h:LANES * (h + 1), :]) for h in range(heads)],
        axis=0).astype(BF16)
    qpe = jnp.concatenate([qp[:, rope * h:rope * (h + 1)] for h in range(heads)], axis=0)
    cc = cc_ref[...].astype(BF16)
    ckt = ckt_ref[...].astype(BF16)
    cn = ckvn_ref[...].astype(BF16)
    kn = kpen_ref[...].astype(BF16)
    s_past = (_dot_nt(qlat, cc) + _dot(qpe, ckt)) * scale
    s_new = (_dot_nt(qlat, cn) + _dot_nt(qpe, kn)) * scale
    m = jnp.maximum(jnp.max(s_past, axis=-1, keepdims=True), jnp.max(s_new, axis=-1, keepdims=True))
    p_past = jnp.exp(s_past - m)
    p_new = jnp.exp(s_new - m)
    l = jnp.sum(p_past, axis=-1, keepdims=True) + jnp.sum(p_new, axis=-1, keepdims=True)
    o_lat = _dot(p_past.astype(BF16), cc) + _dot(p_new.astype(BF16), cn)
    ob = (o_lat * (1.0 / l)).astype(BF16)
    o = jnp.concatenate(
        [_dot(ob[dq * h:dq * (h + 1), :], wuv[:, LANES * h:LANES * (h + 1)]) for h in range(heads)], axis=-1)
    o_ref[...] = o.astype(BF16)


def _mla_sample(qnope, qpe, ckv_new, kpe_new, cache_ckv, cache_kpe_t, layer, w, streams, dq, heads, scale):
    past, lora = cache_ckv.shape[2], cache_ckv.shape[3]
    rope = cache_kpe_t.shape[2]
    n_v = w["w_uv"].shape[2]
    row = lambda b: (b, 0)
    lrow = lambda b: (layer, b, 0)
    body = functools.partial(_mla_sample_body, heads=heads, dq=dq, scale=scale)
    return pl.pallas_call(
        body, grid=(streams,),
        in_specs=[pl.BlockSpec((dq, qnope.shape[1]), row), pl.BlockSpec((dq, qpe.shape[1]), row),
                  pl.BlockSpec((None, dq, lora), lrow), pl.BlockSpec((None, dq, rope), lrow),
                  pl.BlockSpec((None, None, past, lora), lambda b: (layer, b, 0, 0)),
                  pl.BlockSpec((None, None, rope, past), lambda b: (layer, b, 0, 0)),
                  _layer_resident(w["w_ukt"], layer), _layer_resident(w["w_uv"], layer)],
        out_specs=pl.BlockSpec((dq, n_v), row),
        out_shape=jax.ShapeDtypeStruct((streams * dq, n_v), BF16),
        compiler_params=_params(1), name="mla_sample",
    )(qnope, qpe, ckv_new, kpe_new, cache_ckv, cache_kpe_t, w["w_ukt"], w["w_uv"])


def _ret_body(*refs, nb, ts, chunk, heads, dk, dv, has_init, has_prev):
    it = iter(refs)
    x_ref, g_ref, win_ref, gn_ref = next(it), next(it), next(it), next(it)
    cos_ref, sin_ref, dec_ref, xi_ref, zeta_ref, gl_ref = (next(it), next(it), next(it), next(it), next(it),
                                                           next(it))
    if has_init:
        s0_ref = next(it)
    if has_prev:
        next(it)
    y_ref, st_ref = next(it), next(it)

    @pl.when(pl.program_id(1) == 0)
    def _():
        st_ref[...] = s0_ref[...] if has_init else jnp.zeros_like(st_ref)

    d = x_ref.shape[-1]
    half = dk // 2
    k_off, v_off, g_off = heads * dk, 2 * heads * dk, 2 * heads * dk + heads * dv

    def project(x):
        return _dot(_rms(x, g_ref[...]).astype(BF16), win_ref[...])

    shared = chunk < LANES
    if shared:
        qkvg_all = project(x_ref[...].reshape(nb * ts, d))
    for bi in range(nb):
        for r0 in range(0, ts, chunk):
            if shared:
                qkvg = qkvg_all[bi * ts + r0:bi * ts + r0 + chunk, :]
            else:
                qkvg = project(x_ref[bi, r0:r0 + chunk, :])
            cos, sin = cos_ref[r0:r0 + chunk, :], sin_ref[r0:r0 + chunk, :]

            def rot(u):
                u1, u2 = u[:, :half], u[:, half:]
                return jnp.concatenate([u1 * cos - u2 * sin, u1 * sin + u2 * cos], axis=-1)

            ys = []
            for hh in range(heads):
                q = rot(qkvg[:, dk * hh:dk * (hh + 1)])
                k = rot(qkvg[:, k_off + dk * hh:k_off + dk * (hh + 1)]) * (dk ** -0.5)
                v = qkvg[:, v_off + dv * hh:v_off + dv * (hh + 1)].astype(BF16)
                gate = qkvg[:, g_off + dv * hh:g_off + dv * (hh + 1)]
                inner = (_dot_nt(q.astype(BF16), k.astype(BF16)) * dec_ref[hh]).astype(BF16)
                state = st_ref[bi, hh]
                o = _dot(inner, v) + _dot((q * xi_ref[hh]).astype(BF16), state.astype(BF16))
                st_ref[bi, hh] = gl_ref[hh] * state + _dot_tn((k * zeta_ref[hh]).astype(BF16), v)
                ys.append((_silu(gate) * _rms(o, gn_ref[hh])).astype(BF16))
            y_ref[bi, r0:r0 + chunk, :] = jnp.concatenate(ys, axis=-1)


def _ret(x, layer, mix_layer, w, tables, state0, prev, nb, ts):
    chunk = tables[2].shape[1]
    batch, seq, d = x.shape
    heads, dk, dv = w["heads"], w["dk"], w["dv"]
    n_layers = w["w_in"].shape[0]
    cos, sin, dec, xi, zeta, gl = tables
    st_blk = pl.BlockSpec((None, nb, heads, dk, dv), lambda b, s: (layer, b, 0, 0, 0))
    args = [x, w["norm"], w["w_in"], w["gn"], cos, sin, dec, xi, zeta, gl]
    specs = [pl.BlockSpec((nb, ts, d), lambda b, s: (b, s, 0)), _layer_resident(w["norm"], mix_layer),
             _layer_resident(w["w_in"], layer), _layer_resident(w["gn"], layer),
             pl.BlockSpec((ts, LANES), lambda b, s: (s, 0)), pl.BlockSpec((ts, LANES), lambda b, s: (s, 0)),
             _resident(dec.shape), _resident(xi.shape), _resident(zeta.shape), _resident(gl.shape)]
    if state0 is not None:
        args.append(state0)
        specs.append(st_blk)
    aliases = {}
    if prev is not None:
        aliases = {len(args): 1}
        args.append(prev)
        specs.append(pl.BlockSpec(memory_space=pl.ANY))
    body = functools.partial(_ret_body, nb=nb, ts=ts, chunk=chunk, heads=heads, dk=dk, dv=dv,
                             has_init=state0 is not None, has_prev=prev is not None)
    return pl.pallas_call(
        body, grid=(batch // nb, seq // ts), in_specs=specs,
        out_specs=[pl.BlockSpec((nb, ts, heads * dv), lambda b, s: (b, s, 0)), st_blk],
        out_shape=[jax.ShapeDtypeStruct((batch, seq, heads * dv), BF16),
                   jax.ShapeDtypeStruct((n_layers, batch, heads, dk, dv), F32)],
        input_output_aliases=aliases,
        compiler_params=_params(2), name="retention",
    )(*args)


def _rope_tables(pos, dim, width):
    inv = 1.0 / (ROPE_THETA ** (jnp.arange(0, dim, 2, dtype=F32) / dim))
    ang = pos.astype(F32)[:, None] * inv[None, :]
    reps = width // (dim // 2)
    return jnp.tile(jnp.cos(ang), (1, reps)), jnp.tile(jnp.sin(ang), (1, reps))


def _ret_tables(pos, length, heads, dk):
    cos, sin = _rope_tables(pos, dk, dk // 2)
    lg = jnp.log(1.0 - 2.0 ** (-5.0 - jnp.arange(heads, dtype=F32)))
    idx = jnp.arange(length, dtype=F32)
    diff = idx[:, None] - idx[None, :]
    dec = jnp.where(diff >= 0, jnp.exp(lg[:, None, None] * jnp.maximum(diff, 0.0)), 0.0)
    xi = jnp.exp(lg[:, None] * (idx[None, :] + 1.0))[:, :, None]
    zeta = jnp.exp(lg[:, None] * (length - 1.0 - idx[None, :]))[:, :, None]
    gl = jnp.exp(lg * length)[:, None, None]
    return cos, sin, dec, xi, zeta, gl


def _rotate_half_cols(w):
    half = w.shape[-1] // 2
    return jnp.concatenate([-w[..., half:], w[..., :half]], axis=-1)


def _mla_weights(norm, w_dq, q_norm, w_uq, w_dkv, kv_norm, w_uk, w_uv, w_o):
    n, lora, heads, nope = w_uk.shape
    q_lora = w_dq.shape[2]
    rope = w_dkv.shape[2] - lora
    wq = w_uq.reshape(n, q_lora, heads, nope + rope)
    wq_pe = wq[..., nope:]
    w_uq_cat = jnp.concatenate([wq[..., :nope].reshape(n, q_lora, heads * nope),
                                wq_pe.reshape(n, q_lora, heads * rope),
                                _rotate_half_cols(wq_pe).reshape(n, q_lora, heads * rope)], axis=-1)
    wk_pe = w_dkv[..., lora:]
    wk_rot = _rotate_half_cols(wk_pe)
    w_dkv_cat = jnp.concatenate([w_dkv[..., :lora], wk_pe, wk_pe, wk_rot, wk_rot], axis=-1)
    return {
        "norm": norm[:, None, :], "w_dq": w_dq.astype(BF16), "q_norm": q_norm[:, None, :],
        "w_uq": w_uq_cat.astype(BF16), "w_dkv": w_dkv_cat.astype(BF16), "kv_norm": kv_norm[:, None, :],
        "w_ukt": jnp.swapaxes(w_uk.reshape(n, lora, heads * nope), 1, 2).astype(BF16),
        "w_uv": w_uv.reshape(n, lora, -1).astype(BF16), "w_o": w_o.astype(BF16),
        "n_nope": heads * nope, "n_pe": heads * rope, "heads": heads,
        "scale": float(nope + rope) ** -0.5,
    }


def kernel(x_prompt, x_sample, cache_ckv, cache_kpe, state_ret, norm_ffn1, w_ffn1_gate, w_ffn1_up, w_ffn1_down, norm_mix, w_dq, q_norm, w_uq, w_dkv, kv_norm, w_uk, w_uv, w_o_mla, w_ret_in, ret_gn, w_ret_out, norm_ffn2, w_ffn2_gate, w_ffn2_up, w_ffn2_down, norm_final):
    batch, seq, d = x_prompt.shape
    streams, dseq, _ = x_sample.shape
    past = cache_ckv.shape[2]
    depth = norm_ffn1.shape[0]
    n_ret, _, ret_heads, dk, dv = state_ret.shape
    rope = cache_kpe.shape[3]
    assert rope == LANES // 2 and w_uk.shape[3] == LANES and w_uv.shape[3] == LANES
    assert seq % ATTN_Q_TILE == 0 and seq % RET_STEP == 0 and RET_STEP % RET_TILE == 0 and seq % TOKEN_TILE == 0
    assert streams % RET_SAMPLE_STREAMS == 0

    pos_p = jnp.arange(seq)
    pos_s = past + jnp.arange(dseq)
    mla_cos_p, mla_sin_p = _rope_tables(pos_p, rope, LANES)
    mla_cos_s, mla_sin_s = _rope_tables(jnp.tile(pos_s, streams), rope, LANES)
    ret_tab_p = _ret_tables(pos_p, RET_TILE, ret_heads, dk)
    ret_tab_s = _ret_tables(pos_s, dseq, ret_heads, dk)

    ffn1 = (norm_ffn1[:, None, :], w_ffn1_gate.astype(BF16), w_ffn1_up.astype(BF16), w_ffn1_down.astype(BF16))
    ffn2 = (norm_ffn2[:, None, :], w_ffn2_gate.astype(BF16), w_ffn2_up.astype(BF16), w_ffn2_down.astype(BF16))
    mla = _mla_weights(norm_mix[0::2], w_dq, q_norm, w_uq, w_dkv, kv_norm, w_uk, w_uv, w_o_mla)
    ret = {"norm": norm_mix[:, None, :], "w_in": w_ret_in.astype(BF16),
           "gn": ret_gn.reshape(n_ret, ret_heads, 1, dv), "heads": ret_heads, "dk": dk, "dv": dv}
    w_ret_out_b = w_ret_out.astype(BF16)
    final_g = norm_final[None, :]
    cache_kpe_t = jnp.swapaxes(cache_kpe, 2, 3)

    xp = x_prompt.reshape(batch * seq, d)
    xs = x_sample.reshape(streams * dseq, d)
    lat_p = lat_s = st_p = st_s = None
    for i in range(depth):
        j = i // 2
        if i % 2 == 0:
            heads, scale = mla["heads"], mla["scale"]
            xp, qn, qp, c_p, r_p, knt, v, kpt = _ffn(xp, i, *ffn1, mla=(j, mla, mla_cos_p, mla_sin_p, lat_p, seq))
            lat_p = (c_p, r_p)
            o_p = _mla_attn(qn, qp, knt, kpt, v, batch, seq, heads, scale)
            xs, qn, qp, c_s, r_s, _, _, _ = _ffn(xs, i, *ffn1, mla=(j, mla, mla_cos_s, mla_sin_s, lat_s, None))
            lat_s = (c_s, r_s)
            o_s = _mla_sample(qn, qp, c_s, r_s, cache_ckv, cache_kpe_t, j, mla, streams, dseq, heads, scale)
            out_proj = (mla["w_o"], j)
        else:
            xp = _ffn(xp, i, *ffn1)
            xs = _ffn(xs, i, *ffn1)
            o_p, st_p = _ret(xp.reshape(batch, seq, d), j, i, ret, ret_tab_p, None, st_p, 1, RET_STEP)
            o_s, st_s = _ret(xs.reshape(streams, dseq, d), j, i, ret, ret_tab_s, state_ret, st_s,
                             RET_SAMPLE_STREAMS, dseq)
            o_p = o_p.reshape(batch * seq, -1)
            o_s = o_s.reshape(streams * dseq, -1)
            out_proj = (w_ret_out_b, j)
        fg = final_g if i == depth - 1 else None
        xp = _ffn(xp, i, *ffn2, proj=(o_p,) + out_proj, final_g=fg)
        xs = _ffn(xs, i, *ffn2, proj=(o_s,) + out_proj, final_g=fg)
    n_mla = lat_p[0].shape[0]
    return (xp.reshape(batch, seq, d), xs.reshape(streams, dseq, d),
            lat_p[0].reshape(n_mla, batch, seq, -1), jnp.swapaxes(lat_p[1], 2, 3), st_p,
            lat_s[0].reshape(n_mla, streams, dseq, -1), lat_s[1].reshape(n_mla, streams, dseq, -1), st_s)
```

```python
import functools

import jax
import jax.numpy as jnp
from jax import lax
from jax.experimental import pallas as pl
from jax.experimental.pallas import tpu as pltpu

F32 = jnp.float32
BF16 = jnp.bfloat16

EPS = 1e-6
ROPE_THETA = 10000.0
LOG2_E = 1.4426950408889634
CHUNK = 64
LANES = 128
V7X_VMEM_LIMIT_BYTES = 56 * 1024 * 1024

TOKEN_TILE = 1024
TOKEN_TILE_MLA = 512
FFN_GROUP_ROWS = 256
ATTN_Q_TILE = 512
RET_TILE = 256
RET_STEP = 512
RET_SAMPLE_STREAMS = 4
MLA_SAMPLE_STREAMS = 2


def _params(n_axes):
    return pltpu.CompilerParams(dimension_semantics=("arbitrary",) * n_axes,
                                vmem_limit_bytes=V7X_VMEM_LIMIT_BYTES)


def _resident(shape):
    zeros = (0,) * len(shape)
    return pl.BlockSpec(shape, lambda *_: zeros, pipeline_mode=pl.Buffered(1))


def _layer_resident(arr, layer):
    idx = (layer,) + (0,) * (arr.ndim - 1)
    return pl.BlockSpec((None,) + arr.shape[1:], lambda *_: idx, pipeline_mode=pl.Buffered(1))


def _dot(a, b):
    return jnp.dot(a, b, preferred_element_type=F32)


def _dot_nt(a, b):
    return lax.dot_general(a, b, (((1,), (1,)), ((), ())), preferred_element_type=F32)


def _dot_tn(a, b):
    return lax.dot_general(a, b, (((0,), (0,)), ((), ())), preferred_element_type=F32)


def _rms(x, g):
    ms = jnp.mean(x * x, axis=-1, keepdims=True)
    return x * lax.rsqrt(ms + EPS) * g


def _silu(x):
    return x * (1.0 / (1.0 + jnp.exp(-x)))


def _ffn_body(*refs, has_proj, has_final, mla):
    it = iter(refs)
    x_ref = next(it)
    if has_proj:
        o_ref, wp_ref = next(it), next(it)
    g_ref, wg_ref, wu_ref, wd_ref = next(it), next(it), next(it), next(it)
    if has_final:
        gf_ref = next(it)
    if mla is not None:
        mla_w = [next(it) for _ in MLA_WEIGHT_NAMES]
        cos_ref, sin_ref = next(it), next(it)
        if mla["has_prev"]:
            next(it), next(it)
    out_ref = next(it)
    mla_out = [next(it) for _ in range(N_MLA_OUTPUTS)] if mla is not None else None

    sub = min(FFN_GROUP_ROWS, x_ref.shape[0])
    if has_proj:
        mixed = _dot(o_ref[...], wp_ref[...])
    results = []
    for si in range(x_ref.shape[0] // sub):
        rows = slice(si * sub, (si + 1) * sub)
        x = x_ref[rows, :]
        if has_proj:
            x = x + mixed[rows, :]
        h = _rms(x, g_ref[...]).astype(BF16)
        gate = _dot(h, wg_ref[...])
        up = _dot(h, wu_ref[...])
        a = (_silu(gate) * up).astype(BF16)
        r = x + 0.5 * _dot(a, wd_ref[...])
        if has_final:
            r = _rms(r, gf_ref[...])
        out_ref[rows, :] = r
        results.append(r)
    if mla is not None:
        tile = results[0] if len(results) == 1 else jnp.concatenate(results, axis=0)
        _mla_project(tile, mla_w, cos_ref, sin_ref, mla_out, n_nope=mla["n_nope"], n_pe=mla["n_pe"],
                     lora=mla["lora"], kpe_by_feature=mla["kpe_by_feature"])


def _ffn(x, layer, g, wg, wu, wd, proj=None, final_g=None, mla=None):
    t, d = x.shape
    tm = min(TOKEN_TILE if mla is None else TOKEN_TILE_MLA, t)
    assert t % tm == 0
    row = lambda i: (i, 0)
    args, specs = [x], [pl.BlockSpec((tm, d), row)]
    if proj is not None:
        o, wp, wp_layer = proj
        args += [o, wp]
        specs += [pl.BlockSpec((tm, o.shape[1]), row), _layer_resident(wp, wp_layer)]
    args += [g, wg, wu, wd]
    specs += [_layer_resident(a, layer) for a in (g, wg, wu, wd)]
    if final_g is not None:
        args.append(final_g)
        specs.append(_resident(final_g.shape))
    out_specs = [pl.BlockSpec((tm, d), row)]
    out_shape = [jax.ShapeDtypeStruct((t, d), F32)]
    aliases, mla_static = {}, None
    if mla is not None:
        m_layer, w, cos, sin, prev, kpe_seq = mla
        n_layers = w["w_dq"].shape[0]
        n_pos_tiles = cos.shape[0] // tm
        n_nope, n_pe = w["n_nope"], w["n_pe"]
        lora, rope = w["kv_norm"].shape[2], LANES // 2
        n_v = w["w_uv"].shape[2]
        col = lambda i: (0, i)
        lrow = lambda i: (m_layer, i, 0)
        pos = lambda i: (i % n_pos_tiles, 0)
        args += [w[n] for n in MLA_WEIGHT_NAMES] + [cos, sin]
        specs += ([_layer_resident(w[n], m_layer) for n in MLA_WEIGHT_NAMES]
                  + [pl.BlockSpec((tm, LANES), pos), pl.BlockSpec((tm, LANES), pos)])
        if prev is not None:
            aliases = {len(args): 3, len(args) + 1: 4}
            args += list(prev)
            specs += [pl.BlockSpec(memory_space=pl.ANY)] * 2
        if kpe_seq is not None:
            per_seq = kpe_seq // tm
            kpe_spec = pl.BlockSpec((None, None, rope, tm), lambda i: (m_layer, i // per_seq, 0, i % per_seq))
            kpe_shape = jax.ShapeDtypeStruct((n_layers, t // kpe_seq, rope, kpe_seq), F32)
        else:
            kpe_spec = pl.BlockSpec((None, tm, rope), lrow)
            kpe_shape = jax.ShapeDtypeStruct((n_layers, t, rope), F32)
        out_specs += [pl.BlockSpec((tm, n_nope), row), pl.BlockSpec((tm, n_pe), row),
                      pl.BlockSpec((None, tm, lora), lrow), kpe_spec,
                      pl.BlockSpec((n_nope, tm), col), pl.BlockSpec((tm, n_v), row),
                      pl.BlockSpec((LANES, tm), col)]
        out_shape += [jax.ShapeDtypeStruct((t, n_nope), BF16), jax.ShapeDtypeStruct((t, n_pe), BF16),
                      jax.ShapeDtypeStruct((n_layers, t, lora), F32), kpe_shape,
                      jax.ShapeDtypeStruct((n_nope, t), BF16), jax.ShapeDtypeStruct((t, n_v), BF16),
                      jax.ShapeDtypeStruct((LANES, t), BF16)]
        mla_static = {"n_nope": n_nope, "n_pe": n_pe, "lora": lora, "has_prev": prev is not None,
                      "kpe_by_feature": kpe_seq is not None}
    body = functools.partial(_ffn_body, has_proj=proj is not None, has_final=final_g is not None, mla=mla_static)
    outs = pl.pallas_call(
        body, grid=(t // tm,), in_specs=specs, out_specs=out_specs, out_shape=out_shape,
        input_output_aliases=aliases,
        compiler_params=_params(1), name="ffn_half",
    )(*args)
    return outs[0] if mla is None else outs


MLA_WEIGHT_NAMES = ("norm", "w_dq", "q_norm", "w_uq", "w_dkv", "kv_norm", "w_ukt", "w_uv")
N_MLA_OUTPUTS = 7


def _mla_project(x, w_refs, cos_ref, sin_ref, out_refs, *, n_nope, n_pe, lora, kpe_by_feature):
    g_ref, wdq_ref, qn_ref, wuq_ref, wdkv_ref, kvn_ref, wukt_ref, wuv_ref = w_refs
    qnope_ref, qpe_ref, ckv_ref, kpe_ref, knt_ref, v_ref, kpt_ref = out_refs
    reps = n_pe // LANES
    rope = LANES // 2
    h = _rms(x, g_ref[...]).astype(BF16)
    cq = _rms(_dot(h, wdq_ref[...]), qn_ref[...]).astype(BF16)
    q = _dot(cq, wuq_ref[...])
    cos, sin = cos_ref[...], sin_ref[...]
    cos_q = jnp.concatenate([cos] * reps, axis=-1)
    sin_q = jnp.concatenate([sin] * reps, axis=-1)
    qnope_ref[...] = q[:, :n_nope].astype(BF16)
    qpe_ref[...] = (q[:, n_nope:n_nope + n_pe] * cos_q + q[:, n_nope + n_pe:] * sin_q).astype(BF16)
    kv = _dot(h, wdkv_ref[...])
    ckv = _rms(kv[:, :lora], kvn_ref[...])
    ckv_ref[...] = ckv
    kpe2 = kv[:, lora:lora + LANES] * cos + kv[:, lora + LANES:] * sin
    kpe2_t = kpe2.T
    if kpe_by_feature:
        kpe_ref[...] = kpe2_t[:rope, :]
    else:
        kpe_ref[...] = kpe2[:, :rope]
    kpt_ref[...] = kpe2_t.astype(BF16)
    cb = ckv.astype(BF16)
    knt_ref[...] = _dot_nt(wukt_ref[...], cb).astype(BF16)
    v_ref[...] = _dot(cb, wuv_ref[...]).astype(BF16)


def _mla_attn_body(qn_ref, qp_ref, knt_ref, kpt_ref, v_ref, o_ref, kcat_ref, vext_ref, *, seq, tq, exp2_scale):
    head = pl.program_id(1)
    lane = lax.broadcasted_iota(jnp.int32, (tq, LANES), 1)
    keep = (lane >= LANES // 2).astype(jnp.int32) == head % 2
    kcat_ref[:LANES, :] = knt_ref[...]
    kcat_ref[LANES:, :] = kpt_ref[...]
    vext_ref[:, :LANES] = v_ref[...]
    vext_ref[:, LANES:] = jnp.ones((seq, LANES), BF16)
    qc = lax.broadcasted_iota(jnp.int32, (tq, tq), 0) // CHUNK
    kc = lax.broadcasted_iota(jnp.int32, (tq, tq), 1) // CHUNK
    visible = kc <= qc
    for qi in reversed(range(seq // tq)):
        q0 = qi * tq
        qp = jnp.where(keep, qp_ref[q0:q0 + tq, :].astype(F32), 0.0).astype(BF16)
        qh = jnp.concatenate([qn_ref[q0:q0 + tq, :], qp], axis=-1)
        s_d = jnp.where(visible, _dot(qh, kcat_ref[:, q0:q0 + tq]), -jnp.inf)
        m = jnp.max(s_d, axis=-1, keepdims=True)
        if qi > 0:
            s_p = _dot(qh, kcat_ref[:, 0:q0])
            m = jnp.maximum(m, jnp.max(s_p, axis=-1, keepdims=True))
        acc = _dot(jnp.exp2((s_d - m) * exp2_scale).astype(BF16), vext_ref[q0:q0 + tq, :])
        if qi > 0:
            acc = acc + _dot(jnp.exp2((s_p - m) * exp2_scale).astype(BF16), vext_ref[0:q0, :])
        o_ref[q0:q0 + tq, :] = (acc[:, :LANES] * (1.0 / acc[:, LANES:])).astype(BF16)


def _mla_attn(qnope, qpe, knt, kpt, v, batch, seq, heads, scale):
    t = batch * seq
    tq = min(ATTN_Q_TILE, seq)
    body = functools.partial(_mla_attn_body, seq=seq, tq=tq, exp2_scale=scale * LOG2_E)
    blk, blk_t = (seq, LANES), (LANES, seq)
    return pl.pallas_call(
        body, grid=(batch, heads),
        in_specs=[pl.BlockSpec(blk, lambda b, h: (b, h)), pl.BlockSpec(blk, lambda b, h: (b, h // 2)),
                  pl.BlockSpec(blk_t, lambda b, h: (h, b)), pl.BlockSpec(blk_t, lambda b, h: (0, b)),
                  pl.BlockSpec(blk, lambda b, h: (b, h))],
        out_specs=pl.BlockSpec(blk, lambda b, h: (b, h)),
        out_shape=jax.ShapeDtypeStruct((t, heads * LANES), BF16),
        scratch_shapes=[pltpu.VMEM((2 * LANES, seq), BF16), pltpu.VMEM((seq, 2 * LANES), BF16)],
        compiler_params=_params(2), name="mla_attn",
    )(qnope, qpe, knt, kpt, v)


def _mla_sample_body(qn_ref, qp_ref, ckvn_ref, kpen_ref, cc_ref, ckt_ref, wukt_ref, wuv_ref, o_ref,
                     *, nb, heads, dq, scale):
    wukt, wuv = wukt_ref[...], wuv_ref[...]
    rope = LANES // 2
    for bi in range(nb):
        rows = slice(bi * dq, (bi + 1) * dq)
        qn, qp = qn_ref[rows, :], qp_ref[rows, :]
        qlat = jnp.concatenate(
            [_dot(qn[:, LANES * h:LANES * (h + 1)], wukt[LANES * h:LANES * (h + 1), :]) for h in range(heads)],
            axis=0).astype(BF16)
        qpe = jnp.concatenate([qp[:, rope * h:rope * (h + 1)] for h in range(heads)], axis=0)
        cc = cc_ref[bi].astype(BF16)
        ckt = ckt_ref[bi].astype(BF16)
        cn = ckvn_ref[rows, :].astype(BF16)
        kn = kpen_ref[rows, :].astype(BF16)
        s_past = (_dot_nt(qlat, cc) + _dot(qpe, ckt)) * scale
        s_new = (_dot_nt(qlat, cn) + _dot_nt(qpe, kn)) * scale
        m = jnp.maximum(jnp.max(s_past, axis=-1, keepdims=True), jnp.max(s_new, axis=-1, keepdims=True))
        p_past = jnp.exp(s_past - m)
        p_new = jnp.exp(s_new - m)
        l = jnp.sum(p_past, axis=-1, keepdims=True) + jnp.sum(p_new, axis=-1, keepdims=True)
        o_lat = _dot(p_past.astype(BF16), cc) + _dot(p_new.astype(BF16), cn)
        ob = (o_lat * (1.0 / l)).astype(BF16)
        o = jnp.concatenate(
            [_dot(ob[dq * h:dq * (h + 1), :], wuv[:, LANES * h:LANES * (h + 1)]) for h in range(heads)], axis=-1)
        o_ref[rows, :] = o.astype(BF16)


def _mla_sample(qnope, qpe, ckv_new, kpe_new, cache_ckv, cache_kpe_t, layer, w, streams, dq, heads, scale):
    past, lora = cache_ckv.shape[2], cache_ckv.shape[3]
    rope = cache_kpe_t.shape[2]
    n_v = w["w_uv"].shape[2]
    nb = MLA_SAMPLE_STREAMS
    row = lambda b: (b, 0)
    lrow = lambda b: (layer, b, 0)
    body = functools.partial(_mla_sample_body, nb=nb, heads=heads, dq=dq, scale=scale)
    return pl.pallas_call(
        body, grid=(streams // nb,),
        in_specs=[pl.BlockSpec((nb * dq, qnope.shape[1]), row), pl.BlockSpec((nb * dq, qpe.shape[1]), row),
                  pl.BlockSpec((None, nb * dq, lora), lrow), pl.BlockSpec((None, nb * dq, rope), lrow),
                  pl.BlockSpec((None, nb, past, lora), lambda b: (layer, b, 0, 0)),
                  pl.BlockSpec((None, nb, rope, past), lambda b: (layer, b, 0, 0)),
                  _layer_resident(w["w_ukt"], layer), _layer_resident(w["w_uv"], layer)],
        out_specs=pl.BlockSpec((nb * dq, n_v), row),
        out_shape=jax.ShapeDtypeStruct((streams * dq, n_v), BF16),
        compiler_params=_params(1), name="mla_sample",
    )(qnope, qpe, ckv_new, kpe_new, cache_ckv, cache_kpe_t, w["w_ukt"], w["w_uv"])


def _ret_body(*refs, nb, ts, chunk, heads, dk, dv, has_init, has_prev):
    it = iter(refs)
    x_ref, g_ref, win_ref, gn_ref = next(it), next(it), next(it), next(it)
    cos_ref, sin_ref, dec_ref, xi_ref, zeta_ref, gl_ref = (next(it), next(it), next(it), next(it), next(it),
                                                           next(it))
    if has_init:
        s0_ref = next(it)
    if has_prev:
        next(it)
    y_ref, st_ref = next(it), next(it)

    @pl.when(pl.program_id(1) == 0)
    def _():
        st_ref[...] = s0_ref[...] if has_init else jnp.zeros_like(st_ref)

    d = x_ref.shape[-1]
    half = dk // 2
    k_off, v_off, g_off = heads * dk, 2 * heads * dk, 2 * heads * dk + heads * dv

    def project(x):
        return _dot(_rms(x, g_ref[...]).astype(BF16), win_ref[...])

    shared = chunk < LANES
    if shared:
        qkvg_all = project(x_ref[...].reshape(nb * ts, d))
    for bi in range(nb):
        for r0 in range(0, ts, chunk):
            if shared:
                qkvg = qkvg_all[bi * ts + r0:bi * ts + r0 + chunk, :]
            else:
                qkvg = project(x_ref[bi, r0:r0 + chunk, :])
            cos, sin = cos_ref[r0:r0 + chunk, :], sin_ref[r0:r0 + chunk, :]

            def rot(u):
                u1, u2 = u[:, :half], u[:, half:]
                return jnp.concatenate([u1 * cos - u2 * sin, u1 * sin + u2 * cos], axis=-1)

            ys = []
            for hh in range(heads):
                q = rot(qkvg[:, dk * hh:dk * (hh + 1)])
                k = rot(qkvg[:, k_off + dk * hh:k_off + dk * (hh + 1)]) * (dk ** -0.5)
                v = qkvg[:, v_off + dv * hh:v_off + dv * (hh + 1)].astype(BF16)
                gate = qkvg[:, g_off + dv * hh:g_off + dv * (hh + 1)]
                inner = (_dot_nt(q.astype(BF16), k.astype(BF16)) * dec_ref[hh]).astype(BF16)
                state = st_ref[bi, hh]
                o = _dot(inner, v) + _dot((q * xi_ref[hh]).astype(BF16), state.astype(BF16))
                st_ref[bi, hh] = gl_ref[hh] * state + _dot_tn((k * zeta_ref[hh]).astype(BF16), v)
                ys.append((_silu(gate) * _rms(o, gn_ref[hh])).astype(BF16))
            y_ref[bi, r0:r0 + chunk, :] = jnp.concatenate(ys, axis=-1)


def _ret(x, layer, mix_layer, w, tables, state0, prev, nb, ts):
    chunk = tables[2].shape[1]
    batch, seq, d = x.shape
    heads, dk, dv = w["heads"], w["dk"], w["dv"]
    n_layers = w["w_in"].shape[0]
    cos, sin, dec, xi, zeta, gl = tables
    st_blk = pl.BlockSpec((None, nb, heads, dk, dv), lambda b, s: (layer, b, 0, 0, 0))
    args = [x, w["norm"], w["w_in"], w["gn"], cos, sin, dec, xi, zeta, gl]
    specs = [pl.BlockSpec((nb, ts, d), lambda b, s: (b, s, 0)), _layer_resident(w["norm"], mix_layer),
             _layer_resident(w["w_in"], layer), _layer_resident(w["gn"], layer),
             pl.BlockSpec((ts, LANES), lambda b, s: (s, 0)), pl.BlockSpec((ts, LANES), lambda b, s: (s, 0)),
             _resident(dec.shape), _resident(xi.shape), _resident(zeta.shape), _resident(gl.shape)]
    if state0 is not None:
        args.append(state0)
        specs.append(st_blk)
    aliases = {}
    if prev is not None:
        aliases = {len(args): 1}
        args.append(prev)
        specs.append(pl.BlockSpec(memory_space=pl.ANY))
    body = functools.partial(_ret_body, nb=nb, ts=ts, chunk=chunk, heads=heads, dk=dk, dv=dv,
                             has_init=state0 is not None, has_prev=prev is not None)
    return pl.pallas_call(
        body, grid=(batch // nb, seq // ts), in_specs=specs,
        out_specs=[pl.BlockSpec((nb, ts, heads * dv), lambda b, s: (b, s, 0)), st_blk],
        out_shape=[jax.ShapeDtypeStruct((batch, seq, heads * dv), BF16),
                   jax.ShapeDtypeStruct((n_layers, batch, heads, dk, dv), F32)],
        input_output_aliases=aliases,
        compiler_params=_params(2), name="retention",
    )(*args)


def _rope_tables(pos, dim, width):
    inv = 1.0 / (ROPE_THETA ** (jnp.arange(0, dim, 2, dtype=F32) / dim))
    ang = pos.astype(F32)[:, None] * inv[None, :]
    reps = width // (dim // 2)
    return jnp.tile(jnp.cos(ang), (1, reps)), jnp.tile(jnp.sin(ang), (1, reps))


def _ret_tables(pos, length, heads, dk):
    cos, sin = _rope_tables(pos, dk, dk // 2)
    lg = jnp.log(1.0 - 2.0 ** (-5.0 - jnp.arange(heads, dtype=F32)))
    idx = jnp.arange(length, dtype=F32)
    diff = idx[:, None] - idx[None, :]
    dec = jnp.where(diff >= 0, jnp.exp(lg[:, None, None] * jnp.maximum(diff, 0.0)), 0.0)
    xi = jnp.exp(lg[:, None] * (idx[None, :] + 1.0))[:, :, None]
    zeta = jnp.exp(lg[:, None] * (length - 1.0 - idx[None, :]))[:, :, None]
    gl = jnp.exp(lg * length)[:, None, None]
    return cos, sin, dec, xi, zeta, gl


def _rotate_half_cols(w):
    half = w.shape[-1] // 2
    return jnp.concatenate([-w[..., half:], w[..., :half]], axis=-1)


def _mla_weights(norm, w_dq, q_norm, w_uq, w_dkv, kv_norm, w_uk, w_uv, w_o):
    n, lora, heads, nope = w_uk.shape
    q_lora = w_dq.shape[2]
    rope = w_dkv.shape[2] - lora
    wq = w_uq.reshape(n, q_lora, heads, nope + rope)
    wq_pe = wq[..., nope:]
    w_uq_cat = jnp.concatenate([wq[..., :nope].reshape(n, q_lora, heads * nope),
                                wq_pe.reshape(n, q_lora, heads * rope),
                                _rotate_half_cols(wq_pe).reshape(n, q_lora, heads * rope)], axis=-1)
    wk_pe = w_dkv[..., lora:]
    wk_rot = _rotate_half_cols(wk_pe)
    w_dkv_cat = jnp.concatenate([w_dkv[..., :lora], wk_pe, wk_pe, wk_rot, wk_rot], axis=-1)
    return {
        "norm": norm[:, None, :], "w_dq": w_dq.astype(BF16), "q_norm": q_norm[:, None, :],
        "w_uq": w_uq_cat.astype(BF16), "w_dkv": w_dkv_cat.astype(BF16), "kv_norm": kv_norm[:, None, :],
        "w_ukt": jnp.swapaxes(w_uk.reshape(n, lora, heads * nope), 1, 2).astype(BF16),
        "w_uv": w_uv.reshape(n, lora, -1).astype(BF16), "w_o": w_o.astype(BF16),
        "n_nope": heads * nope, "n_pe": heads * rope, "heads": heads,
        "scale": float(nope + rope) ** -0.5,
    }


def kernel(x_prompt, x_sample, cache_ckv, cache_kpe, state_ret, norm_ffn1, w_ffn1_gate, w_ffn1_up, w_ffn1_down, norm_mix, w_dq, q_norm, w_uq, w_dkv, kv_norm, w_uk, w_uv, w_o_mla, w_ret_in, ret_gn, w_ret_out, norm_ffn2, w_ffn2_gate, w_ffn2_up, w_ffn2_down, norm_final):
    batch, seq, d = x_prompt.shape
    streams, dseq, _ = x_sample.shape
    past = cache_ckv.shape[2]
    depth = norm_ffn1.shape[0]
    n_ret, _, ret_heads, dk, dv = state_ret.shape
    rope = cache_kpe.shape[3]
    assert rope == LANES // 2 and w_uk.shape[3] == LANES and w_uv.shape[3] == LANES
    assert seq % ATTN_Q_TILE == 0 and seq % RET_STEP == 0 and RET_STEP % RET_TILE == 0 and seq % TOKEN_TILE_MLA == 0
    assert streams % RET_SAMPLE_STREAMS == 0 and streams % MLA_SAMPLE_STREAMS == 0

    pos_p = jnp.arange(seq)
    pos_s = past + jnp.arange(dseq)
    mla_cos_p, mla_sin_p = _rope_tables(pos_p, rope, LANES)
    mla_cos_s, mla_sin_s = _rope_tables(jnp.tile(pos_s, streams), rope, LANES)
    ret_tab_p = _ret_tables(pos_p, RET_TILE, ret_heads, dk)
    ret_tab_s = _ret_tables(pos_s, dseq, ret_heads, dk)

    ffn1 = (norm_ffn1[:, None, :], w_ffn1_gate.astype(BF16), w_ffn1_up.astype(BF16), w_ffn1_down.astype(BF16))
    ffn2 = (norm_ffn2[:, None, :], w_ffn2_gate.astype(BF16), w_ffn2_up.astype(BF16), w_ffn2_down.astype(BF16))
    mla = _mla_weights(norm_mix[0::2], w_dq, q_norm, w_uq, w_dkv, kv_norm, w_uk, w_uv, w_o_mla)
    ret = {"norm": norm_mix[:, None, :], "w_in": w_ret_in.astype(BF16),
           "gn": ret_gn.reshape(n_ret, ret_heads, 1, dv), "heads": ret_heads, "dk": dk, "dv": dv}
    w_ret_out_b = w_ret_out.astype(BF16)
    final_g = norm_final[None, :]
    cache_kpe_t = jnp.swapaxes(cache_kpe, 2, 3)

    xp = x_prompt.reshape(batch * seq, d)
    xs = x_sample.reshape(streams * dseq, d)
    lat_p = lat_s = st_p = st_s = None
    for i in range(depth):
        j = i // 2
        if i % 2 == 0:
            heads, scale = mla["heads"], mla["scale"]
            xp, qn, qp, c_p, r_p, knt, v, kpt = _ffn(xp, i, *ffn1, mla=(j, mla, mla_cos_p, mla_sin_p, lat_p, seq))
            lat_p = (c_p, r_p)
            o_p = _mla_attn(qn, qp, knt, kpt, v, batch, seq, heads, scale)
            xs, qn, qp, c_s, r_s, _, _, _ = _ffn(xs, i, *ffn1, mla=(j, mla, mla_cos_s, mla_sin_s, lat_s, None))
            lat_s = (c_s, r_s)
            o_s = _mla_sample(qn, qp, c_s, r_s, cache_ckv, cache_kpe_t, j, mla, streams, dseq, heads, scale)
            out_proj = (mla["w_o"], j)
        else:
            xp = _ffn(xp, i, *ffn1)
            xs = _ffn(xs, i, *ffn1)
            o_p, st_p = _ret(xp.reshape(batch, seq, d), j, i, ret, ret_tab_p, None, st_p, 1, RET_STEP)
            o_s, st_s = _ret(xs.reshape(streams, dseq, d), j, i, ret, ret_tab_s, state_ret, st_s,
                             RET_SAMPLE_STREAMS, dseq)
            o_p = o_p.reshape(batch * seq, -1)
            o_s = o_s.reshape(streams * dseq, -1)
            out_proj = (w_ret_out_b, j)
        fg = final_g if i == depth - 1 else None
        xp = _ffn(xp, i, *ffn2, proj=(o_p,) + out_proj, final_g=fg)
        xs = _ffn(xs, i, *ffn2, proj=(o_s,) + out_proj, final_g=fg)
    n_mla = lat_p[0].shape[0]
    return (xp.reshape(batch, seq, d), xs.reshape(streams, dseq, d),
            lat_p[0].reshape(n_mla, batch, seq, -1), jnp.swapaxes(lat_p[1], 2, 3), st_p,
            lat_s[0].reshape(n_mla, streams, dseq, -1), lat_s[1].reshape(n_mla, streams, dseq, -1), st_s)
```

```python
import functools

import jax
import jax.numpy as jnp
from jax import lax
from jax.experimental import pallas as pl
from jax.experimental.pallas import tpu as pltpu

F32 = jnp.float32
BF16 = jnp.bfloat16

EPS = 1e-6
ROPE_THETA = 10000.0
LOG2_E = 1.4426950408889634
CHUNK = 64
LANES = 128
V7X_VMEM_LIMIT_BYTES = 56 * 1024 * 1024

TOKEN_TILE = 1024
TOKEN_TILE_MLA = 512
FFN_GROUP_ROWS = 256
ATTN_Q_TILE = 512
ATTN_HEAD_GROUP = 8
RET_TILE = 256
RET_STEP = 1024
RET_SAMPLE_STREAMS = 4
MLA_SAMPLE_STREAMS = 2


def _params(n_axes):
    return pltpu.CompilerParams(dimension_semantics=("arbitrary",) * n_axes,
                                vmem_limit_bytes=V7X_VMEM_LIMIT_BYTES)


def _resident(shape):
    zeros = (0,) * len(shape)
    return pl.BlockSpec(shape, lambda *_: zeros, pipeline_mode=pl.Buffered(1))


def _layer_resident(arr, layer):
    idx = (layer,) + (0,) * (arr.ndim - 1)
    return pl.BlockSpec((None,) + arr.shape[1:], lambda *_: idx, pipeline_mode=pl.Buffered(1))


def _dot(a, b):
    return jnp.dot(a, b, preferred_element_type=F32)


def _dot_nt(a, b):
    return lax.dot_general(a, b, (((1,), (1,)), ((), ())), preferred_element_type=F32)


def _dot_tn(a, b):
    return lax.dot_general(a, b, (((0,), (0,)), ((), ())), preferred_element_type=F32)


def _rms(x, g):
    ms = jnp.mean(x * x, axis=-1, keepdims=True)
    return x * lax.rsqrt(ms + EPS) * g


def _silu(x):
    return x * (1.0 / (1.0 + jnp.exp(-x)))


def _ffn_body(*refs, has_proj, has_final, mla):
    it = iter(refs)
    x_ref = next(it)
    if has_proj:
        o_ref, wp_ref = next(it), next(it)
    g_ref, wg_ref, wu_ref, wd_ref = next(it), next(it), next(it), next(it)
    if has_final:
        gf_ref = next(it)
    if mla is not None:
        mla_w = [next(it) for _ in MLA_WEIGHT_NAMES]
        cos_ref, sin_ref = next(it), next(it)
        if mla["has_prev"]:
            next(it), next(it)
    out_ref = next(it)
    mla_out = [next(it) for _ in range(N_MLA_OUTPUTS)] if mla is not None else None

    sub = min(FFN_GROUP_ROWS, x_ref.shape[0])
    if has_proj:
        if len(o_ref.shape) == 3:
            o = jnp.concatenate([o_ref[hh] for hh in range(o_ref.shape[0])], axis=-1)
        else:
            o = o_ref[...]
        mixed = _dot(o, wp_ref[...])
    results = []
    for si in range(x_ref.shape[0] // sub):
        rows = slice(si * sub, (si + 1) * sub)
        x = x_ref[rows, :]
        if has_proj:
            x = x + mixed[rows, :]
        h = _rms(x, g_ref[...]).astype(BF16)
        gate = _dot(h, wg_ref[...])
        up = _dot(h, wu_ref[...])
        a = (_silu(gate) * up).astype(BF16)
        r = x + 0.5 * _dot(a, wd_ref[...])
        if has_final:
            r = _rms(r, gf_ref[...])
        out_ref[rows, :] = r
        results.append(r)
    if mla is not None:
        tile = results[0] if len(results) == 1 else jnp.concatenate(results, axis=0)
        _mla_project(tile, mla_w, cos_ref, sin_ref, mla_out, n_nope=mla["n_nope"], n_pe=mla["n_pe"],
                     lora=mla["lora"], kpe_by_feature=mla["kpe_by_feature"])


def _ffn(x, layer, g, wg, wu, wd, proj=None, final_g=None, mla=None):
    t, d = x.shape
    tm = min(TOKEN_TILE if mla is None else TOKEN_TILE_MLA, t)
    assert t % tm == 0
    row = lambda i: (i, 0)
    args, specs = [x], [pl.BlockSpec((tm, d), row)]
    if proj is not None:
        o, wp, wp_layer = proj
        o_spec = (pl.BlockSpec((tm, o.shape[1]), row) if o.ndim == 2
                  else pl.BlockSpec((o.shape[0], tm, o.shape[2]), lambda i: (0, i, 0)))
        args += [o, wp]
        specs += [o_spec, _layer_resident(wp, wp_layer)]
    args += [g, wg, wu, wd]
    specs += [_layer_resident(a, layer) for a in (g, wg, wu, wd)]
    if final_g is not None:
        args.append(final_g)
        specs.append(_resident(final_g.shape))
    out_specs = [pl.BlockSpec((tm, d), row)]
    out_shape = [jax.ShapeDtypeStruct((t, d), F32)]
    aliases, mla_static = {}, None
    if mla is not None:
        m_layer, w, cos, sin, prev, kpe_seq = mla
        n_layers = w["w_dq"].shape[0]
        n_pos_tiles = cos.shape[0] // tm
        n_nope, n_pe = w["n_nope"], w["n_pe"]
        lora, rope = w["kv_norm"].shape[2], LANES // 2
        n_v = w["w_uv"].shape[2]
        col = lambda i: (0, i)
        lrow = lambda i: (m_layer, i, 0)
        pos = lambda i: (i % n_pos_tiles, 0)
        args += [w[n] for n in MLA_WEIGHT_NAMES] + [cos, sin]
        specs += ([_layer_resident(w[n], m_layer) for n in MLA_WEIGHT_NAMES]
                  + [pl.BlockSpec((tm, LANES), pos), pl.BlockSpec((tm, LANES), pos)])
        if prev is not None:
            aliases = {len(args): 3, len(args) + 1: 4}
            args += list(prev)
            specs += [pl.BlockSpec(memory_space=pl.ANY)] * 2
        if kpe_seq is not None:
            per_seq = kpe_seq // tm
            kpe_spec = pl.BlockSpec((None, None, rope, tm), lambda i: (m_layer, i // per_seq, 0, i % per_seq))
            kpe_shape = jax.ShapeDtypeStruct((n_layers, t // kpe_seq, rope, kpe_seq), F32)
        else:
            kpe_spec = pl.BlockSpec((None, tm, rope), lrow)
            kpe_shape = jax.ShapeDtypeStruct((n_layers, t, rope), F32)
        heads, pairs = n_nope // LANES, n_pe // LANES
        assert n_v == n_nope
        hrow = lambda i: (0, i, 0)
        hcol = lambda i: (0, 0, i)
        out_specs += [pl.BlockSpec((heads, tm, LANES), hrow), pl.BlockSpec((pairs, tm, LANES), hrow),
                      pl.BlockSpec((None, tm, lora), lrow), kpe_spec,
                      pl.BlockSpec((heads, LANES, tm), hcol), pl.BlockSpec((heads, tm, LANES), hrow),
                      pl.BlockSpec((LANES, tm), col)]
        out_shape += [jax.ShapeDtypeStruct((heads, t, LANES), BF16), jax.ShapeDtypeStruct((pairs, t, LANES), BF16),
                      jax.ShapeDtypeStruct((n_layers, t, lora), F32), kpe_shape,
                      jax.ShapeDtypeStruct((heads, LANES, t), BF16), jax.ShapeDtypeStruct((heads, t, LANES), BF16),
                      jax.ShapeDtypeStruct((LANES, t), BF16)]
        mla_static = {"n_nope": n_nope, "n_pe": n_pe, "lora": lora, "has_prev": prev is not None,
                      "kpe_by_feature": kpe_seq is not None}
    body = functools.partial(_ffn_body, has_proj=proj is not None, has_final=final_g is not None, mla=mla_static)
    outs = pl.pallas_call(
        body, grid=(t // tm,), in_specs=specs, out_specs=out_specs, out_shape=out_shape,
        input_output_aliases=aliases,
        compiler_params=_params(1), name="ffn_half",
    )(*args)
    return outs[0] if mla is None else outs


MLA_WEIGHT_NAMES = ("norm", "w_dq", "q_norm", "w_uq", "w_dkv", "kv_norm", "w_ukt", "w_uv")
N_MLA_OUTPUTS = 7


def _mla_project(x, w_refs, cos_ref, sin_ref, out_refs, *, n_nope, n_pe, lora, kpe_by_feature):
    g_ref, wdq_ref, qn_ref, wuq_ref, wdkv_ref, kvn_ref, wukt_ref, wuv_ref = w_refs
    qnope_ref, qpe_ref, ckv_ref, kpe_ref, knt_ref, v_ref, kpt_ref = out_refs
    reps = n_pe // LANES
    rope = LANES // 2
    h = _rms(x, g_ref[...]).astype(BF16)
    cq = _rms(_dot(h, wdq_ref[...]), qn_ref[...]).astype(BF16)
    q = _dot(cq, wuq_ref[...])
    cos, sin = cos_ref[...], sin_ref[...]
    cos_q = jnp.concatenate([cos] * reps, axis=-1)
    sin_q = jnp.concatenate([sin] * reps, axis=-1)
    q_nope = q[:, :n_nope].astype(BF16)
    q_pe = (q[:, n_nope:n_nope + n_pe] * cos_q + q[:, n_nope + n_pe:] * sin_q).astype(BF16)
    for hh in range(n_nope // LANES):
        qnope_ref[hh] = q_nope[:, hh * LANES:(hh + 1) * LANES]
    for pp in range(reps):
        qpe_ref[pp] = q_pe[:, pp * LANES:(pp + 1) * LANES]
    kv = _dot(h, wdkv_ref[...])
    ckv = _rms(kv[:, :lora], kvn_ref[...])
    ckv_ref[...] = ckv
    kpe2 = kv[:, lora:lora + LANES] * cos + kv[:, lora + LANES:] * sin
    kpe2_t = kpe2.T
    if kpe_by_feature:
        kpe_ref[...] = kpe2_t[:rope, :]
    else:
        kpe_ref[...] = kpe2[:, :rope]
    kpt_ref[...] = kpe2_t.astype(BF16)
    cb = ckv.astype(BF16)
    k_nope_t = _dot_nt(wukt_ref[...], cb).astype(BF16)
    v = _dot(cb, wuv_ref[...]).astype(BF16)
    for hh in range(n_nope // LANES):
        knt_ref[hh] = k_nope_t[hh * LANES:(hh + 1) * LANES, :]
        v_ref[hh] = v[:, hh * LANES:(hh + 1) * LANES]


def _mla_attn_body(qn_ref, qp_ref, knt_ref, kpt_ref, v_ref, o_ref, kcat_ref, vext_ref, *, seq, tq, group, exp2_scale):
    lane = lax.broadcasted_iota(jnp.int32, (tq, LANES), 1)
    qc = lax.broadcasted_iota(jnp.int32, (tq, tq), 0) // CHUNK
    kc = lax.broadcasted_iota(jnp.int32, (tq, tq), 1) // CHUNK
    visible = kc <= qc
    kcat_ref[LANES:, :] = kpt_ref[...]
    vext_ref[:, LANES:] = jnp.ones((seq, LANES), BF16)

    def one_head(hh, carry):
        keep = (lane >= LANES // 2).astype(jnp.int32) == hh % 2
        kcat_ref[:LANES, :] = knt_ref[hh]
        vext_ref[:, :LANES] = v_ref[hh]
        for qi in reversed(range(seq // tq)):
            q0 = qi * tq
            qp = jnp.where(keep, qp_ref[hh // 2, q0:q0 + tq, :].astype(F32), 0.0).astype(BF16)
            qh = jnp.concatenate([qn_ref[hh, q0:q0 + tq, :], qp], axis=-1)
            s_d = jnp.where(visible, _dot(qh, kcat_ref[:, q0:q0 + tq]), -jnp.inf)
            m = jnp.max(s_d, axis=-1, keepdims=True)
            if qi > 0:
                s_p = _dot(qh, kcat_ref[:, 0:q0])
                m = jnp.maximum(m, jnp.max(s_p, axis=-1, keepdims=True))
            acc = _dot(jnp.exp2((s_d - m) * exp2_scale).astype(BF16), vext_ref[q0:q0 + tq, :])
            if qi > 0:
                acc = acc + _dot(jnp.exp2((s_p - m) * exp2_scale).astype(BF16), vext_ref[0:q0, :])
            o_ref[hh, q0:q0 + tq, :] = (acc[:, :LANES] * (1.0 / acc[:, LANES:])).astype(BF16)
        return carry

    lax.fori_loop(0, group, one_head, 0)


def _mla_attn(qnope, qpe, knt, kpt, v, batch, seq, scale):
    heads, t, _ = qnope.shape
    tq = min(ATTN_Q_TILE, seq)
    group = min(ATTN_HEAD_GROUP, heads)
    assert heads % group == 0 and group % 2 == 0
    body = functools.partial(_mla_attn_body, seq=seq, tq=tq, group=group, exp2_scale=scale * LOG2_E)
    blk = pl.BlockSpec((group, seq, LANES), lambda b, g: (g, b, 0))
    return pl.pallas_call(
        body, grid=(batch, heads // group),
        in_specs=[blk, pl.BlockSpec((group // 2, seq, LANES), lambda b, g: (g, b, 0)),
                  pl.BlockSpec((group, LANES, seq), lambda b, g: (g, 0, b)),
                  pl.BlockSpec((LANES, seq), lambda b, g: (0, b)), blk],
        out_specs=blk,
        out_shape=jax.ShapeDtypeStruct((heads, t, LANES), BF16),
        scratch_shapes=[pltpu.VMEM((2 * LANES, seq), BF16), pltpu.VMEM((seq, 2 * LANES), BF16)],
        compiler_params=_params(2), name="mla_attn",
    )(qnope, qpe, knt, kpt, v)


def _mla_sample_body(qn_ref, qp_ref, ckvn_ref, kpen_ref, cc_ref, ckt_ref, wukt_ref, wuv_ref, o_ref,
                     *, nb, heads, dq, scale):
    wukt, wuv = wukt_ref[...], wuv_ref[...]
    rope = LANES // 2
    for bi in range(nb):
        rows = slice(bi * dq, (bi + 1) * dq)
        qlat = jnp.concatenate(
            [_dot(qn_ref[h, rows, :], wukt[LANES * h:LANES * (h + 1), :]) for h in range(heads)],
            axis=0).astype(BF16)
        qpe = jnp.concatenate(
            [qp_ref[h // 2, rows, :][:, rope * (h % 2):rope * (h % 2 + 1)] for h in range(heads)], axis=0)
        cc = cc_ref[bi].astype(BF16)
        ckt = ckt_ref[bi].astype(BF16)
        cn = ckvn_ref[rows, :].astype(BF16)
        kn = kpen_ref[rows, :].astype(BF16)
        s_past = (_dot_nt(qlat, cc) + _dot(qpe, ckt)) * scale
        s_new = (_dot_nt(qlat, cn) + _dot_nt(qpe, kn)) * scale
        m = jnp.maximum(jnp.max(s_past, axis=-1, keepdims=True), jnp.max(s_new, axis=-1, keepdims=True))
        p_past = jnp.exp(s_past - m)
        p_new = jnp.exp(s_new - m)
        l = jnp.sum(p_past, axis=-1, keepdims=True) + jnp.sum(p_new, axis=-1, keepdims=True)
        o_lat = _dot(p_past.astype(BF16), cc) + _dot(p_new.astype(BF16), cn)
        ob = (o_lat * (1.0 / l)).astype(BF16)
        o = jnp.concatenate(
            [_dot(ob[dq * h:dq * (h + 1), :], wuv[:, LANES * h:LANES * (h + 1)]) for h in range(heads)], axis=-1)
        o_ref[rows, :] = o.astype(BF16)


def _mla_sample(qnope, qpe, ckv_new, kpe_new, cache_ckv, cache_kpe_t, layer, w, streams, dq, heads, scale):
    past, lora = cache_ckv.shape[2], cache_ckv.shape[3]
    rope = cache_kpe_t.shape[2]
    n_v = w["w_uv"].shape[2]
    nb = MLA_SAMPLE_STREAMS
    row = lambda b: (b, 0)
    lrow = lambda b: (layer, b, 0)
    body = functools.partial(_mla_sample_body, nb=nb, heads=heads, dq=dq, scale=scale)
    return pl.pallas_call(
        body, grid=(streams // nb,),
        in_specs=[pl.BlockSpec((heads, nb * dq, LANES), lambda b: (0, b, 0)),
                  pl.BlockSpec((heads // 2, nb * dq, LANES), lambda b: (0, b, 0)),
                  pl.BlockSpec((None, nb * dq, lora), lrow), pl.BlockSpec((None, nb * dq, rope), lrow),
                  pl.BlockSpec((None, nb, past, lora), lambda b: (layer, b, 0, 0)),
                  pl.BlockSpec((None, nb, rope, past), lambda b: (layer, b, 0, 0)),
                  _layer_resident(w["w_ukt"], layer), _layer_resident(w["w_uv"], layer)],
        out_specs=pl.BlockSpec((nb * dq, n_v), row),
        out_shape=jax.ShapeDtypeStruct((streams * dq, n_v), BF16),
        compiler_params=_params(1), name="mla_sample",
    )(qnope, qpe, ckv_new, kpe_new, cache_ckv, cache_kpe_t, w["w_ukt"], w["w_uv"])


def _ret_body(*refs, nb, ts, chunk, heads, dk, dv, has_init, has_prev):
    it = iter(refs)
    x_ref, g_ref, win_ref, gn_ref = next(it), next(it), next(it), next(it)
    cos_ref, sin_ref, dec_ref, xi_ref, zeta_ref, gl_ref = (next(it), next(it), next(it), next(it), next(it),
                                                           next(it))
    if has_init:
        s0_ref = next(it)
    if has_prev:
        next(it)
    y_ref, st_ref = next(it), next(it)

    @pl.when(pl.program_id(1) == 0)
    def _():
        st_ref[...] = s0_ref[...] if has_init else jnp.zeros_like(st_ref)

    d = x_ref.shape[-1]
    half = dk // 2
    k_off, v_off, g_off = heads * dk, 2 * heads * dk, 2 * heads * dk + heads * dv

    def project(x):
        return _dot(_rms(x, g_ref[...]).astype(BF16), win_ref[...])

    shared = chunk < LANES
    if shared:
        qkvg_all = project(x_ref[...].reshape(nb * ts, d))
    for bi in range(nb):
        for r0 in range(0, ts, chunk):
            if shared:
                qkvg = qkvg_all[bi * ts + r0:bi * ts + r0 + chunk, :]
            else:
                qkvg = project(x_ref[bi, r0:r0 + chunk, :])
            cos, sin = cos_ref[r0:r0 + chunk, :], sin_ref[r0:r0 + chunk, :]

            def rot(u):
                u1, u2 = u[:, :half], u[:, half:]
                return jnp.concatenate([u1 * cos - u2 * sin, u1 * sin + u2 * cos], axis=-1)

            ys = []
            for hh in range(heads):
                q = rot(qkvg[:, dk * hh:dk * (hh + 1)])
                k = rot(qkvg[:, k_off + dk * hh:k_off + dk * (hh + 1)]) * (dk ** -0.5)
                v = qkvg[:, v_off + dv * hh:v_off + dv * (hh + 1)].astype(BF16)
                gate = qkvg[:, g_off + dv * hh:g_off + dv * (hh + 1)]
                inner = (_dot_nt(q.astype(BF16), k.astype(BF16)) * dec_ref[hh]).astype(BF16)
                state = st_ref[bi, hh]
                o = _dot(inner, v) + _dot((q * xi_ref[hh]).astype(BF16), state.astype(BF16))
                st_ref[bi, hh] = gl_ref[hh] * state + _dot_tn((k * zeta_ref[hh]).astype(BF16), v)
                ys.append((_silu(gate) * _rms(o, gn_ref[hh])).astype(BF16))
            y_ref[bi, r0:r0 + chunk, :] = jnp.concatenate(ys, axis=-1)


def _ret(x, layer, mix_layer, w, tables, state0, prev, nb, ts):
    chunk = tables[2].shape[1]
    batch, seq, d = x.shape
    heads, dk, dv = w["heads"], w["dk"], w["dv"]
    n_layers = w["w_in"].shape[0]
    cos, sin, dec, xi, zeta, gl = tables
    st_blk = pl.BlockSpec((None, nb, heads, dk, dv), lambda b, s: (layer, b, 0, 0, 0))
    args = [x, w["norm"], w["w_in"], w["gn"], cos, sin, dec, xi, zeta, gl]
    specs = [pl.BlockSpec((nb, ts, d), lambda b, s: (b, s, 0)), _layer_resident(w["norm"], mix_layer),
             _layer_resident(w["w_in"], layer), _layer_resident(w["gn"], layer),
             pl.BlockSpec((ts, LANES), lambda b, s: (s, 0)), pl.BlockSpec((ts, LANES), lambda b, s: (s, 0)),
             _resident(dec.shape), _resident(xi.shape), _resident(zeta.shape), _resident(gl.shape)]
    if state0 is not None:
        args.append(state0)
        specs.append(st_blk)
    aliases = {}
    if prev is not None:
        aliases = {len(args): 1}
        args.append(prev)
        specs.append(pl.BlockSpec(memory_space=pl.ANY))
    body = functools.partial(_ret_body, nb=nb, ts=ts, chunk=chunk, heads=heads, dk=dk, dv=dv,
                             has_init=state0 is not None, has_prev=prev is not None)
    return pl.pallas_call(
        body, grid=(batch // nb, seq // ts), in_specs=specs,
        out_specs=[pl.BlockSpec((nb, ts, heads * dv), lambda b, s: (b, s, 0)), st_blk],
        out_shape=[jax.ShapeDtypeStruct((batch, seq, heads * dv), BF16),
                   jax.ShapeDtypeStruct((n_layers, batch, heads, dk, dv), F32)],
        input_output_aliases=aliases,
        compiler_params=_params(2), name="retention",
    )(*args)


def _rope_tables(pos, dim, width):
    inv = 1.0 / (ROPE_THETA ** (jnp.arange(0, dim, 2, dtype=F32) / dim))
    ang = pos.astype(F32)[:, None] * inv[None, :]
    reps = width // (dim // 2)
    return jnp.tile(jnp.cos(ang), (1, reps)), jnp.tile(jnp.sin(ang), (1, reps))


def _ret_tables(pos, length, heads, dk):
    cos, sin = _rope_tables(pos, dk, dk // 2)
    lg = jnp.log(1.0 - 2.0 ** (-5.0 - jnp.arange(heads, dtype=F32)))
    idx = jnp.arange(length, dtype=F32)
    diff = idx[:, None] - idx[None, :]
    dec = jnp.where(diff >= 0, jnp.exp(lg[:, None, None] * jnp.maximum(diff, 0.0)), 0.0)
    xi = jnp.exp(lg[:, None] * (idx[None, :] + 1.0))[:, :, None]
    zeta = jnp.exp(lg[:, None] * (length - 1.0 - idx[None, :]))[:, :, None]
    gl = jnp.exp(lg * length)[:, None, None]
    return cos, sin, dec, xi, zeta, gl


def _rotate_half_cols(w):
    half = w.shape[-1] // 2
    return jnp.concatenate([-w[..., half:], w[..., :half]], axis=-1)


def _mla_weights(norm, w_dq, q_norm, w_uq, w_dkv, kv_norm, w_uk, w_uv, w_o):
    n, lora, heads, nope = w_uk.shape
    q_lora = w_dq.shape[2]
    rope = w_dkv.shape[2] - lora
    wq = w_uq.reshape(n, q_lora, heads, nope + rope)
    wq_pe = wq[..., nope:]
    w_uq_cat = jnp.concatenate([wq[..., :nope].reshape(n, q_lora, heads * nope),
                                wq_pe.reshape(n, q_lora, heads * rope),
                                _rotate_half_cols(wq_pe).reshape(n, q_lora, heads * rope)], axis=-1)
    wk_pe = w_dkv[..., lora:]
    wk_rot = _rotate_half_cols(wk_pe)
    w_dkv_cat = jnp.concatenate([w_dkv[..., :lora], wk_pe, wk_pe, wk_rot, wk_rot], axis=-1)
    return {
        "norm": norm[:, None, :], "w_dq": w_dq.astype(BF16), "q_norm": q_norm[:, None, :],
        "w_uq": w_uq_cat.astype(BF16), "w_dkv": w_dkv_cat.astype(BF16), "kv_norm": kv_norm[:, None, :],
        "w_ukt": jnp.swapaxes(w_uk.reshape(n, lora, heads * nope), 1, 2).astype(BF16),
        "w_uv": w_uv.reshape(n, lora, -1).astype(BF16), "w_o": w_o.astype(BF16),
        "n_nope": heads * nope, "n_pe": heads * rope, "heads": heads,
        "scale": float(nope + rope) ** -0.5,
    }


def kernel(x_prompt, x_sample, cache_ckv, cache_kpe, state_ret, norm_ffn1, w_ffn1_gate, w_ffn1_up, w_ffn1_down, norm_mix, w_dq, q_norm, w_uq, w_dkv, kv_norm, w_uk, w_uv, w_o_mla, w_ret_in, ret_gn, w_ret_out, norm_ffn2, w_ffn2_gate, w_ffn2_up, w_ffn2_down, norm_final):
    batch, seq, d = x_prompt.shape
    streams, dseq, _ = x_sample.shape
    past = cache_ckv.shape[2]
    depth = norm_ffn1.shape[0]
    n_ret, _, ret_heads, dk, dv = state_ret.shape
    rope = cache_kpe.shape[3]
    assert rope == LANES // 2 and w_uk.shape[3] == LANES and w_uv.shape[3] == LANES
    assert seq % ATTN_Q_TILE == 0 and seq % RET_STEP == 0 and RET_STEP % RET_TILE == 0 and seq % TOKEN_TILE_MLA == 0
    assert streams % RET_SAMPLE_STREAMS == 0 and streams % MLA_SAMPLE_STREAMS == 0

    pos_p = jnp.arange(seq)
    pos_s = past + jnp.arange(dseq)
    mla_cos_p, mla_sin_p = _rope_tables(pos_p, rope, LANES)
    mla_cos_s, mla_sin_s = _rope_tables(jnp.tile(pos_s, streams), rope, LANES)
    ret_tab_p = _ret_tables(pos_p, RET_TILE, ret_heads, dk)
    ret_tab_s = _ret_tables(pos_s, dseq, ret_heads, dk)

    ffn1 = (norm_ffn1[:, None, :], w_ffn1_gate.astype(BF16), w_ffn1_up.astype(BF16), w_ffn1_down.astype(BF16))
    ffn2 = (norm_ffn2[:, None, :], w_ffn2_gate.astype(BF16), w_ffn2_up.astype(BF16), w_ffn2_down.astype(BF16))
    mla = _mla_weights(norm_mix[0::2], w_dq, q_norm, w_uq, w_dkv, kv_norm, w_uk, w_uv, w_o_mla)
    ret = {"norm": norm_mix[:, None, :], "w_in": w_ret_in.astype(BF16),
           "gn": ret_gn.reshape(n_ret, ret_heads, 1, dv), "heads": ret_heads, "dk": dk, "dv": dv}
    w_ret_out_b = w_ret_out.astype(BF16)
    final_g = norm_final[None, :]
    cache_kpe_t = jnp.swapaxes(cache_kpe, 2, 3)

    xp = x_prompt.reshape(batch * seq, d)
    xs = x_sample.reshape(streams * dseq, d)
    lat_p = lat_s = st_p = st_s = None
    for i in range(depth):
        j = i // 2
        if i % 2 == 0:
            heads, scale = mla["heads"], mla["scale"]
            xp, qn, qp, c_p, r_p, knt, v, kpt = _ffn(xp, i, *ffn1, mla=(j, mla, mla_cos_p, mla_sin_p, lat_p, seq))
            lat_p = (c_p, r_p)
            o_p = _mla_attn(qn, qp, knt, kpt, v, batch, seq, scale)
            xs, qn, qp, c_s, r_s, _, _, _ = _ffn(xs, i, *ffn1, mla=(j, mla, mla_cos_s, mla_sin_s, lat_s, None))
            lat_s = (c_s, r_s)
            o_s = _mla_sample(qn, qp, c_s, r_s, cache_ckv, cache_kpe_t, j, mla, streams, dseq, heads, scale)
            out_proj = (mla["w_o"], j)
        else:
            xp = _ffn(xp, i, *ffn1)
            xs = _ffn(xs, i, *ffn1)
            o_p, st_p = _ret(xp.reshape(batch, seq, d), j, i, ret, ret_tab_p, None, st_p, 1, RET_STEP)
            o_s, st_s = _ret(xs.reshape(streams, dseq, d), j, i, ret, ret_tab_s, state_ret, st_s,
                             RET_SAMPLE_STREAMS, dseq)
            o_p = o_p.reshape(batch * seq, -1)
            o_s = o_s.reshape(streams * dseq, -1)
            out_proj = (w_ret_out_b, j)
        fg = final_g if i == depth - 1 else None
        xp = _ffn(xp, i, *ffn2, proj=(o_p,) + out_proj, final_g=fg)
        xs = _ffn(xs, i, *ffn2, proj=(o_s,) + out_proj, final_g=fg)
    n_mla = lat_p[0].shape[0]
    return (xp.reshape(batch, seq, d), xs.reshape(streams, dseq, d),
            lat_p[0].reshape(n_mla, batch, seq, -1), jnp.swapaxes(lat_p[1], 2, 3), st_p,
            lat_s[0].reshape(n_mla, streams, dseq, -1), lat_s[1].reshape(n_mla, streams, dseq, -1), st_s)
```

```python
import functools

import jax
import jax.numpy as jnp
from jax import lax
from jax.experimental import pallas as pl
from jax.experimental.pallas import tpu as pltpu

F32 = jnp.float32
BF16 = jnp.bfloat16

EPS = 1e-6
ROPE_THETA = 10000.0
LOG2_E = 1.4426950408889634
CHUNK = 64
LANES = 128
V7X_VMEM_LIMIT_BYTES = 56 * 1024 * 1024

TOKEN_TILE = 1024
TOKEN_TILE_MLA = 512
FFN_GROUP_ROWS = 256
ATTN_Q_TILE = 512
ATTN_HEAD_GROUP = 8
RET_TILE = 256
RET_STEP = 512
RET_SAMPLE_STREAMS = 4
MLA_SAMPLE_STREAMS = 2


def _params(n_axes):
    return pltpu.CompilerParams(dimension_semantics=("arbitrary",) * n_axes,
                                vmem_limit_bytes=V7X_VMEM_LIMIT_BYTES)


def _resident(shape):
    zeros = (0,) * len(shape)
    return pl.BlockSpec(shape, lambda *_: zeros, pipeline_mode=pl.Buffered(1))


def _layer_resident(arr, layer):
    idx = (layer,) + (0,) * (arr.ndim - 1)
    return pl.BlockSpec((None,) + arr.shape[1:], lambda *_: idx, pipeline_mode=pl.Buffered(1))


def _dot(a, b):
    return jnp.dot(a, b, preferred_element_type=F32)


def _dot_nt(a, b):
    return lax.dot_general(a, b, (((1,), (1,)), ((), ())), preferred_element_type=F32)


def _dot_tn(a, b):
    return lax.dot_general(a, b, (((0,), (0,)), ((), ())), preferred_element_type=F32)


def _rms(x, g):
    ms = jnp.mean(x * x, axis=-1, keepdims=True)
    return x * lax.rsqrt(ms + EPS) * g


def _silu(x):
    return x * (1.0 / (1.0 + jnp.exp(-x)))


def _ffn_body(*refs, has_proj, has_final, mla):
    it = iter(refs)
    x_ref = next(it)
    if has_proj:
        o_ref, wp_ref = next(it), next(it)
    g_ref, wg_ref, wu_ref, wd_ref = next(it), next(it), next(it), next(it)
    if has_final:
        gf_ref = next(it)
    if mla is not None:
        mla_w = [next(it) for _ in MLA_WEIGHT_NAMES]
        cos_ref, sin_ref = next(it), next(it)
        if mla["has_prev"]:
            next(it), next(it)
    out_ref = next(it)
    mla_out = [next(it) for _ in range(N_MLA_OUTPUTS)] if mla is not None else None

    sub = min(FFN_GROUP_ROWS, x_ref.shape[0])
    if has_proj:
        if len(o_ref.shape) == 3:
            o = jnp.concatenate([o_ref[hh] for hh in range(o_ref.shape[0])], axis=-1)
        else:
            o = o_ref[...]
        mixed = _dot(o, wp_ref[...])
    results = []
    for si in range(x_ref.shape[0] // sub):
        rows = slice(si * sub, (si + 1) * sub)
        x = x_ref[rows, :]
        if has_proj:
            x = x + mixed[rows, :]
        h = _rms(x, g_ref[...]).astype(BF16)
        gate = _dot(h, wg_ref[...])
        up = _dot(h, wu_ref[...])
        a = (_silu(gate) * up).astype(BF16)
        r = x + 0.5 * _dot(a, wd_ref[...])
        if has_final:
            r = _rms(r, gf_ref[...])
        out_ref[rows, :] = r
        results.append(r)
    if mla is not None:
        tile = results[0] if len(results) == 1 else jnp.concatenate(results, axis=0)
        _mla_project(tile, mla_w, cos_ref, sin_ref, mla_out, n_nope=mla["n_nope"], n_pe=mla["n_pe"],
                     lora=mla["lora"], kpe_by_feature=mla["kpe_by_feature"])


def _ffn(x, layer, g, wg, wu, wd, proj=None, final_g=None, mla=None):
    t, d = x.shape
    tm = min(TOKEN_TILE if mla is None else TOKEN_TILE_MLA, t)
    assert t % tm == 0
    row = lambda i: (i, 0)
    args, specs = [x], [pl.BlockSpec((tm, d), row)]
    if proj is not None:
        o, wp, wp_layer = proj
        o_spec = (pl.BlockSpec((tm, o.shape[1]), row) if o.ndim == 2
                  else pl.BlockSpec((o.shape[0], tm, o.shape[2]), lambda i: (0, i, 0)))
        args += [o, wp]
        specs += [o_spec, _layer_resident(wp, wp_layer)]
    args += [g, wg, wu, wd]
    specs += [_layer_resident(a, layer) for a in (g, wg, wu, wd)]
    if final_g is not None:
        args.append(final_g)
        specs.append(_resident(final_g.shape))
    out_specs = [pl.BlockSpec((tm, d), row)]
    out_shape = [jax.ShapeDtypeStruct((t, d), F32)]
    aliases, mla_static = {}, None
    if mla is not None:
        m_layer, w, cos, sin, prev, kpe_seq = mla
        n_layers = w["w_dq"].shape[0]
        n_pos_tiles = cos.shape[0] // tm
        n_nope, n_pe = w["n_nope"], w["n_pe"]
        lora, rope = w["kv_norm"].shape[2], LANES // 2
        n_v = w["w_uv"].shape[2]
        col = lambda i: (0, i)
        lrow = lambda i: (m_layer, i, 0)
        pos = lambda i: (i % n_pos_tiles, 0)
        args += [w[n] for n in MLA_WEIGHT_NAMES] + [cos, sin]
        specs += ([_layer_resident(w[n], m_layer) for n in MLA_WEIGHT_NAMES]
                  + [pl.BlockSpec((tm, LANES), pos), pl.BlockSpec((tm, LANES), pos)])
        if prev is not None:
            aliases = {len(args): 3, len(args) + 1: 4}
            args += list(prev)
            specs += [pl.BlockSpec(memory_space=pl.ANY)] * 2
        if kpe_seq is not None:
            per_seq = kpe_seq // tm
            kpe_spec = pl.BlockSpec((None, None, rope, tm), lambda i: (m_layer, i // per_seq, 0, i % per_seq))
            kpe_shape = jax.ShapeDtypeStruct((n_layers, t // kpe_seq, rope, kpe_seq), F32)
        else:
            kpe_spec = pl.BlockSpec((None, tm, rope), lrow)
            kpe_shape = jax.ShapeDtypeStruct((n_layers, t, rope), F32)
        heads, pairs = n_nope // LANES, n_pe // LANES
        assert n_v == n_nope
        hrow = lambda i: (0, i, 0)
        hcol = lambda i: (0, 0, i)
        out_specs += [pl.BlockSpec((heads, tm, LANES), hrow), pl.BlockSpec((pairs, tm, LANES), hrow),
                      pl.BlockSpec((None, tm, lora), lrow), kpe_spec,
                      pl.BlockSpec((heads, LANES, tm), hcol), pl.BlockSpec((heads, tm, LANES), hrow),
                      pl.BlockSpec((LANES, tm), col)]
        out_shape += [jax.ShapeDtypeStruct((heads, t, LANES), BF16), jax.ShapeDtypeStruct((pairs, t, LANES), BF16),
                      jax.ShapeDtypeStruct((n_layers, t, lora), F32), kpe_shape,
                      jax.ShapeDtypeStruct((heads, LANES, t), BF16), jax.ShapeDtypeStruct((heads, t, LANES), BF16),
                      jax.ShapeDtypeStruct((LANES, t), BF16)]
        mla_static = {"n_nope": n_nope, "n_pe": n_pe, "lora": lora, "has_prev": prev is not None,
                      "kpe_by_feature": kpe_seq is not None}
    body = functools.partial(_ffn_body, has_proj=proj is not None, has_final=final_g is not None, mla=mla_static)
    outs = pl.pallas_call(
        body, grid=(t // tm,), in_specs=specs, out_specs=out_specs, out_shape=out_shape,
        input_output_aliases=aliases,
        compiler_params=_params(1), name="ffn_half",
    )(*args)
    return outs[0] if mla is None else outs


MLA_WEIGHT_NAMES = ("norm", "w_dq", "q_norm", "w_uq", "w_dkv", "kv_norm", "w_ukt", "w_uv")
N_MLA_OUTPUTS = 7


def _mla_project(x, w_refs, cos_ref, sin_ref, out_refs, *, n_nope, n_pe, lora, kpe_by_feature):
    g_ref, wdq_ref, qn_ref, wuq_ref, wdkv_ref, kvn_ref, wukt_ref, wuv_ref = w_refs
    qnope_ref, qpe_ref, ckv_ref, kpe_ref, knt_ref, v_ref, kpt_ref = out_refs
    reps = n_pe // LANES
    rope = LANES // 2
    h = _rms(x, g_ref[...]).astype(BF16)
    cq = _rms(_dot(h, wdq_ref[...]), qn_ref[...]).astype(BF16)
    q = _dot(cq, wuq_ref[...])
    cos, sin = cos_ref[...], sin_ref[...]
    cos_q = jnp.concatenate([cos] * reps, axis=-1)
    sin_q = jnp.concatenate([sin] * reps, axis=-1)
    q_nope = q[:, :n_nope].astype(BF16)
    q_pe = (q[:, n_nope:n_nope + n_pe] * cos_q + q[:, n_nope + n_pe:] * sin_q).astype(BF16)
    for hh in range(n_nope // LANES):
        qnope_ref[hh] = q_nope[:, hh * LANES:(hh + 1) * LANES]
    for pp in range(reps):
        qpe_ref[pp] = q_pe[:, pp * LANES:(pp + 1) * LANES]
    kv = _dot(h, wdkv_ref[...])
    ckv = _rms(kv[:, :lora], kvn_ref[...])
    ckv_ref[...] = ckv
    kpe2 = kv[:, lora:lora + LANES] * cos + kv[:, lora + LANES:] * sin
    kpe2_t = kpe2.T
    if kpe_by_feature:
        kpe_ref[...] = kpe2_t[:rope, :]
    else:
        kpe_ref[...] = kpe2[:, :rope]
    kpt_ref[...] = kpe2_t.astype(BF16)
    cb = ckv.astype(BF16)
    k_nope_t = _dot_nt(wukt_ref[...], cb).astype(BF16)
    v = _dot(cb, wuv_ref[...]).astype(BF16)
    for hh in range(n_nope // LANES):
        knt_ref[hh] = k_nope_t[hh * LANES:(hh + 1) * LANES, :]
        v_ref[hh] = v[:, hh * LANES:(hh + 1) * LANES]


def _mla_attn_body(qn_ref, qp_ref, knt_ref, kpt_ref, v_ref, o_ref, kcat_ref, vext_ref, *, seq, tq, group, exp2_scale):
    lane = lax.broadcasted_iota(jnp.int32, (tq, LANES), 1)
    qc = lax.broadcasted_iota(jnp.int32, (tq, tq), 0) // CHUNK
    kc = lax.broadcasted_iota(jnp.int32, (tq, tq), 1) // CHUNK
    visible = kc <= qc
    for slot in range(2):
        kcat_ref[slot, LANES:, :] = kpt_ref[...]
        vext_ref[slot, :, LANES:] = jnp.ones((seq, LANES), BF16)

    def head_pair(pp, carry):
        for slot in range(2):
            hh = 2 * pp + slot
            keep = (lane >= LANES // 2) == bool(slot)
            kcat_ref[slot, :LANES, :] = knt_ref[hh]
            vext_ref[slot, :, :LANES] = v_ref[hh]
            for qi in reversed(range(seq // tq)):
                q0 = qi * tq
                qp = jnp.where(keep, qp_ref[pp, q0:q0 + tq, :].astype(F32), 0.0).astype(BF16)
                qh = jnp.concatenate([qn_ref[hh, q0:q0 + tq, :], qp], axis=-1)
                s_d = jnp.where(visible, _dot(qh, kcat_ref[slot, :, q0:q0 + tq]), -jnp.inf)
                m = jnp.max(s_d, axis=-1, keepdims=True)
                if qi > 0:
                    s_p = _dot(qh, kcat_ref[slot, :, 0:q0])
                    m = jnp.maximum(m, jnp.max(s_p, axis=-1, keepdims=True))
                acc = _dot(jnp.exp2((s_d - m) * exp2_scale).astype(BF16), vext_ref[slot, q0:q0 + tq, :])
                if qi > 0:
                    acc = acc + _dot(jnp.exp2((s_p - m) * exp2_scale).astype(BF16), vext_ref[slot, 0:q0, :])
                o_ref[hh, q0:q0 + tq, :] = (acc[:, :LANES] * (1.0 / acc[:, LANES:])).astype(BF16)
        return carry

    lax.fori_loop(0, group // 2, head_pair, 0)


def _mla_attn(qnope, qpe, knt, kpt, v, batch, seq, scale):
    heads, t, _ = qnope.shape
    tq = min(ATTN_Q_TILE, seq)
    group = min(ATTN_HEAD_GROUP, heads)
    assert heads % group == 0 and group % 2 == 0
    body = functools.partial(_mla_attn_body, seq=seq, tq=tq, group=group, exp2_scale=scale * LOG2_E)
    blk = pl.BlockSpec((group, seq, LANES), lambda b, g: (g, b, 0))
    return pl.pallas_call(
        body, grid=(batch, heads // group),
        in_specs=[blk, pl.BlockSpec((group // 2, seq, LANES), lambda b, g: (g, b, 0)),
                  pl.BlockSpec((group, LANES, seq), lambda b, g: (g, 0, b)),
                  pl.BlockSpec((LANES, seq), lambda b, g: (0, b)), blk],
        out_specs=blk,
        out_shape=jax.ShapeDtypeStruct((heads, t, LANES), BF16),
        scratch_shapes=[pltpu.VMEM((2, 2 * LANES, seq), BF16), pltpu.VMEM((2, seq, 2 * LANES), BF16)],
        compiler_params=_params(2), name="mla_attn",
    )(qnope, qpe, knt, kpt, v)


def _mla_sample_body(qn_ref, qp_ref, ckvn_ref, kpen_ref, cc_ref, ckt_ref, wukt_ref, wuv_ref, o_ref,
                     *, nb, heads, dq, scale):
    wukt, wuv = wukt_ref[...], wuv_ref[...]
    rope = LANES // 2
    for bi in range(nb):
        rows = slice(bi * dq, (bi + 1) * dq)
        qlat = jnp.concatenate(
            [_dot(qn_ref[h, rows, :], wukt[LANES * h:LANES * (h + 1), :]) for h in range(heads)],
            axis=0).astype(BF16)
        qpe = jnp.concatenate(
            [qp_ref[h // 2, rows, :][:, rope * (h % 2):rope * (h % 2 + 1)] for h in range(heads)], axis=0)
        cc = cc_ref[bi].astype(BF16)
        ckt = ckt_ref[bi].astype(BF16)
        cn = ckvn_ref[rows, :].astype(BF16)
        kn = kpen_ref[rows, :].astype(BF16)
        s_past = (_dot_nt(qlat, cc) + _dot(qpe, ckt)) * scale
        s_new = (_dot_nt(qlat, cn) + _dot_nt(qpe, kn)) * scale
        m = jnp.maximum(jnp.max(s_past, axis=-1, keepdims=True), jnp.max(s_new, axis=-1, keepdims=True))
        p_past = jnp.exp(s_past - m)
        p_new = jnp.exp(s_new - m)
        l = jnp.sum(p_past, axis=-1, keepdims=True) + jnp.sum(p_new, axis=-1, keepdims=True)
        o_lat = _dot(p_past.astype(BF16), cc) + _dot(p_new.astype(BF16), cn)
        ob = (o_lat * (1.0 / l)).astype(BF16)
        o = jnp.concatenate(
            [_dot(ob[dq * h:dq * (h + 1), :], wuv[:, LANES * h:LANES * (h + 1)]) for h in range(heads)], axis=-1)
        o_ref[rows, :] = o.astype(BF16)


def _mla_sample(qnope, qpe, ckv_new, kpe_new, cache_ckv, cache_kpe_t, layer, w, streams, dq, heads, scale):
    past, lora = cache_ckv.shape[2], cache_ckv.shape[3]
    rope = cache_kpe_t.shape[2]
    n_v = w["w_uv"].shape[2]
    nb = MLA_SAMPLE_STREAMS
    row = lambda b: (b, 0)
    lrow = lambda b: (layer, b, 0)
    body = functools.partial(_mla_sample_body, nb=nb, heads=heads, dq=dq, scale=scale)
    return pl.pallas_call(
        body, grid=(streams // nb,),
        in_specs=[pl.BlockSpec((heads, nb * dq, LANES), lambda b: (0, b, 0)),
                  pl.BlockSpec((heads // 2, nb * dq, LANES), lambda b: (0, b, 0)),
                  pl.BlockSpec((None, nb * dq, lora), lrow), pl.BlockSpec((None, nb * dq, rope), lrow),
                  pl.BlockSpec((None, nb, past, lora), lambda b: (layer, b, 0, 0)),
                  pl.BlockSpec((None, nb, rope, past), lambda b: (layer, b, 0, 0)),
                  _layer_resident(w["w_ukt"], layer), _layer_resident(w["w_uv"], layer)],
        out_specs=pl.BlockSpec((nb * dq, n_v), row),
        out_shape=jax.ShapeDtypeStruct((streams * dq, n_v), BF16),
        compiler_params=_params(1), name="mla_sample",
    )(qnope, qpe, ckv_new, kpe_new, cache_ckv, cache_kpe_t, w["w_ukt"], w["w_uv"])


def _ret_body(*refs, nb, ts, chunk, heads, dk, dv, has_init, has_prev):
    it = iter(refs)
    x_ref, g_ref, win_ref, gn_ref = next(it), next(it), next(it), next(it)
    cos_ref, sin_ref, dec_ref, xi_ref, zeta_ref, gl_ref = (next(it), next(it), next(it), next(it), next(it),
                                                           next(it))
    if has_init:
        s0_ref = next(it)
    if has_prev:
        next(it)
    y_ref, st_ref = next(it), next(it)

    @pl.when(pl.program_id(1) == 0)
    def _():
        st_ref[...] = s0_ref[...] if has_init else jnp.zeros_like(st_ref)

    d = x_ref.shape[-1]
    half = dk // 2
    k_off, v_off, g_off = heads * dk, 2 * heads * dk, 2 * heads * dk + heads * dv

    def project(x):
        return _dot(_rms(x, g_ref[...]).astype(BF16), win_ref[...])

    shared = chunk < LANES
    if shared:
        qkvg_all = project(x_ref[...].reshape(nb * ts, d))
    for bi in range(nb):
        for r0 in range(0, ts, chunk):
            if shared:
                qkvg = qkvg_all[bi * ts + r0:bi * ts + r0 + chunk, :]
            else:
                qkvg = project(x_ref[bi, r0:r0 + chunk, :])
            cos, sin = cos_ref[r0:r0 + chunk, :], sin_ref[r0:r0 + chunk, :]

            def rot(u):
                u1, u2 = u[:, :half], u[:, half:]
                return jnp.concatenate([u1 * cos - u2 * sin, u1 * sin + u2 * cos], axis=-1)

            ys = []
            for hh in range(heads):
                q = rot(qkvg[:, dk * hh:dk * (hh + 1)])
                k = rot(qkvg[:, k_off + dk * hh:k_off + dk * (hh + 1)]) * (dk ** -0.5)
                v = qkvg[:, v_off + dv * hh:v_off + dv * (hh + 1)].astype(BF16)
                gate = qkvg[:, g_off + dv * hh:g_off + dv * (hh + 1)]
                inner = (_dot_nt(q.astype(BF16), k.astype(BF16)) * dec_ref[hh]).astype(BF16)
                state = st_ref[bi, hh]
                o = _dot(inner, v) + _dot((q * xi_ref[hh]).astype(BF16), state.astype(BF16))
                st_ref[bi, hh] = gl_ref[hh] * state + _dot_tn((k * zeta_ref[hh]).astype(BF16), v)
                ys.append((_silu(gate) * _rms(o, gn_ref[hh])).astype(BF16))
            y_ref[bi, r0:r0 + chunk, :] = jnp.concatenate(ys, axis=-1)


def _ret(x, layer, mix_layer, w, tables, state0, prev, nb, ts):
    chunk = tables[2].shape[1]
    batch, seq, d = x.shape
    heads, dk, dv = w["heads"], w["dk"], w["dv"]
    n_layers = w["w_in"].shape[0]
    cos, sin, dec, xi, zeta, gl = tables
    st_blk = pl.BlockSpec((None, nb, heads, dk, dv), lambda b, s: (layer, b, 0, 0, 0))
    args = [x, w["norm"], w["w_in"], w["gn"], cos, sin, dec, xi, zeta, gl]
    specs = [pl.BlockSpec((nb, ts, d), lambda b, s: (b, s, 0)), _layer_resident(w["norm"], mix_layer),
             _layer_resident(w["w_in"], layer), _layer_resident(w["gn"], layer),
             pl.BlockSpec((ts, LANES), lambda b, s: (s, 0)), pl.BlockSpec((ts, LANES), lambda b, s: (s, 0)),
             _resident(dec.shape), _resident(xi.shape), _resident(zeta.shape), _resident(gl.shape)]
    if state0 is not None:
        args.append(state0)
        specs.append(st_blk)
    aliases = {}
    if prev is not None:
        aliases = {len(args): 1}
        args.append(prev)
        specs.append(pl.BlockSpec(memory_space=pl.ANY))
    body = functools.partial(_ret_body, nb=nb, ts=ts, chunk=chunk, heads=heads, dk=dk, dv=dv,
                             has_init=state0 is not None, has_prev=prev is not None)
    return pl.pallas_call(
        body, grid=(batch // nb, seq // ts), in_specs=specs,
        out_specs=[pl.BlockSpec((nb, ts, heads * dv), lambda b, s: (b, s, 0)), st_blk],
        out_shape=[jax.ShapeDtypeStruct((batch, seq, heads * dv), BF16),
                   jax.ShapeDtypeStruct((n_layers, batch, heads, dk, dv), F32)],
        input_output_aliases=aliases,
        compiler_params=_params(2), name="retention",
    )(*args)


def _rope_tables(pos, dim, width):
    inv = 1.0 / (ROPE_THETA ** (jnp.arange(0, dim, 2, dtype=F32) / dim))
    ang = pos.astype(F32)[:, None] * inv[None, :]
    reps = width // (dim // 2)
    return jnp.tile(jnp.cos(ang), (1, reps)), jnp.tile(jnp.sin(ang), (1, reps))


def _ret_tables(pos, length, heads, dk):
    cos, sin = _rope_tables(pos, dk, dk // 2)
    lg = jnp.log(1.0 - 2.0 ** (-5.0 - jnp.arange(heads, dtype=F32)))
    idx = jnp.arange(length, dtype=F32)
    diff = idx[:, None] - idx[None, :]
    dec = jnp.where(diff >= 0, jnp.exp(lg[:, None, None] * jnp.maximum(diff, 0.0)), 0.0)
    xi = jnp.exp(lg[:, None] * (idx[None, :] + 1.0))[:, :, None]
    zeta = jnp.exp(lg[:, None] * (length - 1.0 - idx[None, :]))[:, :, None]
    gl = jnp.exp(lg * length)[:, None, None]
    return cos, sin, dec, xi, zeta, gl


def _rotate_half_cols(w):
    half = w.shape[-1] // 2
    return jnp.concatenate([-w[..., half:], w[..., :half]], axis=-1)


def _mla_weights(norm, w_dq, q_norm, w_uq, w_dkv, kv_norm, w_uk, w_uv, w_o):
    n, lora, heads, nope = w_uk.shape
    q_lora = w_dq.shape[2]
    rope = w_dkv.shape[2] - lora
    wq = w_uq.reshape(n, q_lora, heads, nope + rope)
    wq_pe = wq[..., nope:]
    w_uq_cat = jnp.concatenate([wq[..., :nope].reshape(n, q_lora, heads * nope),
                                wq_pe.reshape(n, q_lora, heads * rope),
                                _rotate_half_cols(wq_pe).reshape(n, q_lora, heads * rope)], axis=-1)
    wk_pe = w_dkv[..., lora:]
    wk_rot = _rotate_half_cols(wk_pe)
    w_dkv_cat = jnp.concatenate([w_dkv[..., :lora], wk_pe, wk_pe, wk_rot, wk_rot], axis=-1)
    return {
        "norm": norm[:, None, :], "w_dq": w_dq.astype(BF16), "q_norm": q_norm[:, None, :],
        "w_uq": w_uq_cat.astype(BF16), "w_dkv": w_dkv_cat.astype(BF16), "kv_norm": kv_norm[:, None, :],
        "w_ukt": jnp.swapaxes(w_uk.reshape(n, lora, heads * nope), 1, 2).astype(BF16),
        "w_uv": w_uv.reshape(n, lora, -1).astype(BF16), "w_o": w_o.astype(BF16),
        "n_nope": heads * nope, "n_pe": heads * rope, "heads": heads,
        "scale": float(nope + rope) ** -0.5,
    }


def kernel(x_prompt, x_sample, cache_ckv, cache_kpe, state_ret, norm_ffn1, w_ffn1_gate, w_ffn1_up, w_ffn1_down, norm_mix, w_dq, q_norm, w_uq, w_dkv, kv_norm, w_uk, w_uv, w_o_mla, w_ret_in, ret_gn, w_ret_out, norm_ffn2, w_ffn2_gate, w_ffn2_up, w_ffn2_down, norm_final):
    batch, seq, d = x_prompt.shape
    streams, dseq, _ = x_sample.shape
    past = cache_ckv.shape[2]
    depth = norm_ffn1.shape[0]
    n_ret, _, ret_heads, dk, dv = state_ret.shape
    rope = cache_kpe.shape[3]
    assert rope == LANES // 2 and w_uk.shape[3] == LANES and w_uv.shape[3] == LANES
    assert seq % ATTN_Q_TILE == 0 and seq % RET_STEP == 0 and RET_STEP % RET_TILE == 0 and seq % TOKEN_TILE_MLA == 0
    assert streams % RET_SAMPLE_STREAMS == 0 and streams % MLA_SAMPLE_STREAMS == 0

    pos_p = jnp.arange(seq)
    pos_s = past + jnp.arange(dseq)
    mla_cos_p, mla_sin_p = _rope_tables(pos_p, rope, LANES)
    mla_cos_s, mla_sin_s = _rope_tables(jnp.tile(pos_s, streams), rope, LANES)
    ret_tab_p = _ret_tables(pos_p, RET_TILE, ret_heads, dk)
    ret_tab_s = _ret_tables(pos_s, dseq, ret_heads, dk)

    ffn1 = (norm_ffn1[:, None, :], w_ffn1_gate.astype(BF16), w_ffn1_up.astype(BF16), w_ffn1_down.astype(BF16))
    ffn2 = (norm_ffn2[:, None, :], w_ffn2_gate.astype(BF16), w_ffn2_up.astype(BF16), w_ffn2_down.astype(BF16))
    mla = _mla_weights(norm_mix[0::2], w_dq, q_norm, w_uq, w_dkv, kv_norm, w_uk, w_uv, w_o_mla)
    ret = {"norm": norm_mix[:, None, :], "w_in": w_ret_in.astype(BF16),
           "gn": ret_gn.reshape(n_ret, ret_heads, 1, dv), "heads": ret_heads, "dk": dk, "dv": dv}
    w_ret_out_b = w_ret_out.astype(BF16)
    final_g = norm_final[None, :]
    cache_kpe_t = jnp.swapaxes(cache_kpe, 2, 3)

    xp = x_prompt.reshape(batch * seq, d)
    xs = x_sample.reshape(streams * dseq, d)
    lat_p = lat_s = st_p = st_s = None
    for i in range(depth):
        j = i // 2
        if i % 2 == 0:
            heads, scale = mla["heads"], mla["scale"]
            xp, qn, qp, c_p, r_p, knt, v, kpt = _ffn(xp, i, *ffn1, mla=(j, mla, mla_cos_p, mla_sin_p, lat_p, seq))
            lat_p = (c_p, r_p)
            o_p = _mla_attn(qn, qp, knt, kpt, v, batch, seq, scale)
            xs, qn, qp, c_s, r_s, _, _, _ = _ffn(xs, i, *ffn1, mla=(j, mla, mla_cos_s, mla_sin_s, lat_s, None))
            lat_s = (c_s, r_s)
            o_s = _mla_sample(qn, qp, c_s, r_s, cache_ckv, cache_kpe_t, j, mla, streams, dseq, heads, scale)
            out_proj = (mla["w_o"], j)
        else:
            xp = _ffn(xp, i, *ffn1)
            xs = _ffn(xs, i, *ffn1)
            o_p, st_p = _ret(xp.reshape(batch, seq, d), j, i, ret, ret_tab_p, None, st_p, 1, RET_STEP)
            o_s, st_s = _ret(xs.reshape(streams, dseq, d), j, i, ret, ret_tab_s, state_ret, st_s,
                             RET_SAMPLE_STREAMS, dseq)
            o_p = o_p.reshape(batch * seq, -1)
            o_s = o_s.reshape(streams * dseq, -1)
            out_proj = (w_ret_out_b, j)
        fg = final_g if i == depth - 1 else None
        xp = _ffn(xp, i, *ffn2, proj=(o_p,) + out_proj, final_g=fg)
        xs = _ffn(xs, i, *ffn2, proj=(o_s,) + out_proj, final_g=fg)
    n_mla = lat_p[0].shape[0]
    return (xp.reshape(batch, seq, d), xs.reshape(streams, dseq, d),
            lat_p[0].reshape(n_mla, batch, seq, -1), jnp.swapaxes(lat_p[1], 2, 3), st_p,
            lat_s[0].reshape(n_mla, streams, dseq, -1), lat_s[1].reshape(n_mla, streams, dseq, -1), st_s)
```

```python
import functools

import jax
import jax.numpy as jnp
from jax import lax
from jax.experimental import pallas as pl
from jax.experimental.pallas import tpu as pltpu

F32 = jnp.float32
BF16 = jnp.bfloat16

EPS = 1e-6
ROPE_THETA = 10000.0
LOG2_E = 1.4426950408889634
CHUNK = 64
LANES = 128
V7X_VMEM_LIMIT_BYTES = 56 * 1024 * 1024

TOKEN_TILE = 1024
TOKEN_TILE_MLA = 512
FFN_GROUP_ROWS = 256
ATTN_Q_TILE = 512
ATTN_HEAD_GROUP = 8
RET_TILE = 256
RET_STEP = 512
RET_SAMPLE_STREAMS = 4
MLA_SAMPLE_STREAMS = 2


def _params(n_axes):
    return pltpu.CompilerParams(dimension_semantics=("arbitrary",) * n_axes,
                                vmem_limit_bytes=V7X_VMEM_LIMIT_BYTES)


def _resident(shape):
    zeros = (0,) * len(shape)
    return pl.BlockSpec(shape, lambda *_: zeros, pipeline_mode=pl.Buffered(1))


def _layer_resident(arr, layer):
    idx = (layer,) + (0,) * (arr.ndim - 1)
    return pl.BlockSpec((None,) + arr.shape[1:], lambda *_: idx, pipeline_mode=pl.Buffered(1))


def _dot(a, b):
    return jnp.dot(a, b, preferred_element_type=F32)


def _dot_nt(a, b):
    return lax.dot_general(a, b, (((1,), (1,)), ((), ())), preferred_element_type=F32)


def _dot_tn(a, b):
    return lax.dot_general(a, b, (((0,), (0,)), ((), ())), preferred_element_type=F32)


def _rms(x, g):
    ms = jnp.mean(x * x, axis=-1, keepdims=True)
    return x * lax.rsqrt(ms + EPS) * g


def _silu(x):
    return x * (1.0 / (1.0 + jnp.exp(-x)))


def _ffn_body(*refs, has_proj, has_final, mla):
    it = iter(refs)
    x_ref = next(it)
    if has_proj:
        o_ref, wp_ref = next(it), next(it)
    g_ref, wg_ref, wu_ref, wd_ref = next(it), next(it), next(it), next(it)
    if has_final:
        gf_ref = next(it)
    if mla is not None:
        mla_w = [next(it) for _ in MLA_WEIGHT_NAMES]
        cos_ref, sin_ref = next(it), next(it)
        if mla["has_prev"]:
            next(it), next(it)
    out_ref = next(it)
    mla_out = [next(it) for _ in range(N_MLA_OUTPUTS)] if mla is not None else None

    sub = min(FFN_GROUP_ROWS, x_ref.shape[0])
    if has_proj:
        if len(o_ref.shape) == 3:
            o = jnp.concatenate([o_ref[hh] for hh in range(o_ref.shape[0])], axis=-1)
        else:
            o = o_ref[...]
        mixed = _dot(o, wp_ref[...])
    results = []
    for si in range(x_ref.shape[0] // sub):
        rows = slice(si * sub, (si + 1) * sub)
        x = x_ref[rows, :]
        if has_proj:
            x = x + mixed[rows, :]
        h = _rms(x, g_ref[...]).astype(BF16)
        gate = _dot(h, wg_ref[...])
        up = _dot(h, wu_ref[...])
        a = (_silu(gate) * up).astype(BF16)
        r = x + 0.5 * _dot(a, wd_ref[...])
        if has_final:
            r = _rms(r, gf_ref[...])
        out_ref[rows, :] = r
        results.append(r)
    if mla is not None:
        tile = results[0] if len(results) == 1 else jnp.concatenate(results, axis=0)
        _mla_project(tile, mla_w, cos_ref, sin_ref, mla_out, n_nope=mla["n_nope"], n_pe=mla["n_pe"],
                     lora=mla["lora"], kpe_by_feature=mla["kpe_by_feature"])


def _ffn(x, layer, g, wg, wu, wd, proj=None, final_g=None, mla=None):
    t, d = x.shape
    tm = min(TOKEN_TILE if mla is None else TOKEN_TILE_MLA, t)
    assert t % tm == 0
    row = lambda i: (i, 0)
    args, specs = [x], [pl.BlockSpec((tm, d), row)]
    if proj is not None:
        o, wp, wp_layer = proj
        o_spec = (pl.BlockSpec((tm, o.shape[1]), row) if o.ndim == 2
                  else pl.BlockSpec((o.shape[0], tm, o.shape[2]), lambda i: (0, i, 0)))
        args += [o, wp]
        specs += [o_spec, _layer_resident(wp, wp_layer)]
    args += [g, wg, wu, wd]
    specs += [_layer_resident(a, layer) for a in (g, wg, wu, wd)]
    if final_g is not None:
        args.append(final_g)
        specs.append(_resident(final_g.shape))
    out_specs = [pl.BlockSpec((tm, d), row)]
    out_shape = [jax.ShapeDtypeStruct((t, d), F32)]
    aliases, mla_static = {}, None
    if mla is not None:
        aliases, mla_static = _mla_plumbing(mla, t, tm, args, specs, out_specs, out_shape)
    body = functools.partial(_ffn_body, has_proj=proj is not None, has_final=final_g is not None, mla=mla_static)
    outs = pl.pallas_call(
        body, grid=(t // tm,), in_specs=specs, out_specs=out_specs, out_shape=out_shape,
        input_output_aliases=aliases,
        compiler_params=_params(1), name="ffn_half",
    )(*args)
    return outs[0] if mla is None else outs


MLA_WEIGHT_NAMES = ("norm", "w_dq", "q_norm", "w_uq", "w_dkv", "kv_norm", "w_ukt", "w_uv")
N_MLA_OUTPUTS = 7


def _mla_project(x, w_refs, cos_ref, sin_ref, out_refs, *, n_nope, n_pe, lora, kpe_by_feature):
    g_ref, wdq_ref, qn_ref, wuq_ref, wdkv_ref, kvn_ref, wukt_ref, wuv_ref = w_refs
    qnope_ref, qpe_ref, ckv_ref, kpe_ref, knt_ref, v_ref, kpt_ref = out_refs
    rope = LANES // 2
    h = _rms(x, g_ref[...]).astype(BF16)
    cq = _rms(_dot(h, wdq_ref[...]), qn_ref[...]).astype(BF16)
    q = _dot(cq, wuq_ref[...])
    cos, sin = cos_ref[...], sin_ref[...]
    first_half = lax.broadcasted_iota(jnp.int32, cos.shape, 1) % rope < rope // 2

    def rotary(u):
        below = pltpu.roll(u, rope // 2, 1)
        above = pltpu.roll(u, LANES - rope // 2, 1)
        return u * cos + jnp.where(first_half, -above, below) * sin

    for hh in range(n_nope // LANES):
        qnope_ref[hh] = q[:, hh * LANES:(hh + 1) * LANES].astype(BF16)
    for pp in range(n_pe // LANES):
        qpe_ref[pp] = rotary(q[:, n_nope + pp * LANES:n_nope + (pp + 1) * LANES]).astype(BF16)
    kv = _dot(h, wdkv_ref[...])
    ckv = _rms(kv[:, :lora], kvn_ref[...])
    ckv_ref[...] = ckv
    kpe2 = rotary(kv[:, lora:])
    kpe2_t = kpe2.T
    if kpe_by_feature:
        kpe_ref[...] = kpe2_t[:rope, :]
    else:
        kpe_ref[...] = kpe2[:, :rope]
    kpt_ref[...] = kpe2_t.astype(BF16)
    cb = ckv.astype(BF16)
    k_nope_t = _dot_nt(wukt_ref[...], cb).astype(BF16)
    v = _dot(cb, wuv_ref[...]).astype(BF16)
    for hh in range(n_nope // LANES):
        knt_ref[hh] = k_nope_t[hh * LANES:(hh + 1) * LANES, :]
        v_ref[hh] = v[:, hh * LANES:(hh + 1) * LANES]


def _mla_plumbing(mla, t, tm, args, specs, out_specs, out_shape):
    m_layer, w, cos, sin, prev, kpe_seq = mla
    n_layers = w["w_dq"].shape[0]
    n_pos_tiles = cos.shape[0] // tm
    n_nope, n_pe = w["n_nope"], w["n_pe"]
    lora, rope = w["kv_norm"].shape[2], LANES // 2
    assert w["w_uv"].shape[2] == n_nope
    col = lambda i: (0, i)
    lrow = lambda i: (m_layer, i, 0)
    pos = lambda i: (i % n_pos_tiles, 0)
    args += [w[n] for n in MLA_WEIGHT_NAMES] + [cos, sin]
    specs += ([_layer_resident(w[n], m_layer) for n in MLA_WEIGHT_NAMES]
              + [pl.BlockSpec((tm, LANES), pos), pl.BlockSpec((tm, LANES), pos)])
    aliases = {}
    if prev is not None:
        first = len(out_specs)
        aliases = {len(args): first + 2, len(args) + 1: first + 3}
        args += list(prev)
        specs += [pl.BlockSpec(memory_space=pl.ANY)] * 2
    if kpe_seq is not None:
        per_seq = kpe_seq // tm
        kpe_spec = pl.BlockSpec((None, None, rope, tm), lambda i: (m_layer, i // per_seq, 0, i % per_seq))
        kpe_shape = jax.ShapeDtypeStruct((n_layers, t // kpe_seq, rope, kpe_seq), F32)
    else:
        kpe_spec = pl.BlockSpec((None, tm, rope), lrow)
        kpe_shape = jax.ShapeDtypeStruct((n_layers, t, rope), F32)
    heads, pairs = n_nope // LANES, n_pe // LANES
    hrow = lambda i: (0, i, 0)
    hcol = lambda i: (0, 0, i)
    out_specs += [pl.BlockSpec((heads, tm, LANES), hrow), pl.BlockSpec((pairs, tm, LANES), hrow),
                  pl.BlockSpec((None, tm, lora), lrow), kpe_spec,
                  pl.BlockSpec((heads, LANES, tm), hcol), pl.BlockSpec((heads, tm, LANES), hrow),
                  pl.BlockSpec((LANES, tm), col)]
    out_shape += [jax.ShapeDtypeStruct((heads, t, LANES), BF16), jax.ShapeDtypeStruct((pairs, t, LANES), BF16),
                  jax.ShapeDtypeStruct((n_layers, t, lora), F32), kpe_shape,
                  jax.ShapeDtypeStruct((heads, LANES, t), BF16), jax.ShapeDtypeStruct((heads, t, LANES), BF16),
                  jax.ShapeDtypeStruct((LANES, t), BF16)]
    return aliases, {"n_nope": n_nope, "n_pe": n_pe, "lora": lora, "has_prev": prev is not None,
                     "kpe_by_feature": kpe_seq is not None}


def _mla_proj_body(x_ref, *refs, mla):
    n_w = len(MLA_WEIGHT_NAMES)
    _mla_project(x_ref[...], refs[:n_w], refs[n_w], refs[n_w + 1], refs[n_w + 2 + 2 * mla["has_prev"]:],
                 n_nope=mla["n_nope"], n_pe=mla["n_pe"], lora=mla["lora"], kpe_by_feature=mla["kpe_by_feature"])


def _mla_proj(x, mla):
    t, d = x.shape
    tm = min(TOKEN_TILE, t)
    assert t % tm == 0
    args, specs, out_specs, out_shape = [x], [pl.BlockSpec((tm, d), lambda i: (i, 0))], [], []
    aliases, static = _mla_plumbing(mla, t, tm, args, specs, out_specs, out_shape)
    return pl.pallas_call(
        functools.partial(_mla_proj_body, mla=static), grid=(t // tm,), in_specs=specs,
        out_specs=out_specs, out_shape=out_shape, input_output_aliases=aliases,
        compiler_params=_params(1), name="mla_proj",
    )(*args)


def _mla_attn_body(qn_ref, qp_ref, knt_ref, kpt_ref, v_ref, o_ref, kcat_ref, vext_ref, *, seq, tq, group, exp2_scale):
    lane = lax.broadcasted_iota(jnp.int32, (tq, LANES), 1)
    qc = lax.broadcasted_iota(jnp.int32, (tq, tq), 0) // CHUNK
    kc = lax.broadcasted_iota(jnp.int32, (tq, tq), 1) // CHUNK
    visible = kc <= qc
    for slot in range(2):
        kcat_ref[slot, LANES:, :] = kpt_ref[...]
        vext_ref[slot, :, LANES:] = jnp.ones((seq, LANES), BF16)

    def head_pair(pp, carry):
        for slot in range(2):
            hh = 2 * pp + slot
            keep = (lane >= LANES // 2) == bool(slot)
            kcat_ref[slot, :LANES, :] = knt_ref[hh]
            vext_ref[slot, :, :LANES] = v_ref[hh]
            for qi in reversed(range(seq // tq)):
                q0 = qi * tq
                qp = jnp.where(keep, qp_ref[pp, q0:q0 + tq, :].astype(F32), 0.0).astype(BF16)
                qh = jnp.concatenate([qn_ref[hh, q0:q0 + tq, :], qp], axis=-1)
                s_d = jnp.where(visible, _dot(qh, kcat_ref[slot, :, q0:q0 + tq]), -jnp.inf)
                m = jnp.max(s_d, axis=-1, keepdims=True)
                if qi > 0:
                    s_p = _dot(qh, kcat_ref[slot, :, 0:q0])
                    m = jnp.maximum(m, jnp.max(s_p, axis=-1, keepdims=True))
                acc = _dot(jnp.exp2((s_d - m) * exp2_scale).astype(BF16), vext_ref[slot, q0:q0 + tq, :])
                if qi > 0:
                    acc = acc + _dot(jnp.exp2((s_p - m) * exp2_scale).astype(BF16), vext_ref[slot, 0:q0, :])
                o_ref[hh, q0:q0 + tq, :] = (acc[:, :LANES] * (1.0 / acc[:, LANES:])).astype(BF16)
        return carry

    lax.fori_loop(0, group // 2, head_pair, 0)


def _mla_attn(qnope, qpe, knt, kpt, v, batch, seq, scale):
    heads, t, _ = qnope.shape
    tq = min(ATTN_Q_TILE, seq)
    group = min(ATTN_HEAD_GROUP, heads)
    assert heads % group == 0 and group % 2 == 0
    body = functools.partial(_mla_attn_body, seq=seq, tq=tq, group=group, exp2_scale=scale * LOG2_E)
    blk = pl.BlockSpec((group, seq, LANES), lambda b, g: (g, b, 0))
    return pl.pallas_call(
        body, grid=(batch, heads // group),
        in_specs=[blk, pl.BlockSpec((group // 2, seq, LANES), lambda b, g: (g, b, 0)),
                  pl.BlockSpec((group, LANES, seq), lambda b, g: (g, 0, b)),
                  pl.BlockSpec((LANES, seq), lambda b, g: (0, b)), blk],
        out_specs=blk,
        out_shape=jax.ShapeDtypeStruct((heads, t, LANES), BF16),
        scratch_shapes=[pltpu.VMEM((2, 2 * LANES, seq), BF16), pltpu.VMEM((2, seq, 2 * LANES), BF16)],
        compiler_params=_params(2), name="mla_attn",
    )(qnope, qpe, knt, kpt, v)


def _mla_sample_body(qn_ref, qp_ref, ckvn_ref, kpen_ref, cc_ref, ckt_ref, wukt_ref, wuv_ref, o_ref,
                     *, nb, heads, dq, scale):
    wukt, wuv = wukt_ref[...], wuv_ref[...]
    rope = LANES // 2
    for bi in range(nb):
        rows = slice(bi * dq, (bi + 1) * dq)
        qlat = jnp.concatenate(
            [_dot(qn_ref[h, rows, :], wukt[LANES * h:LANES * (h + 1), :]) for h in range(heads)],
            axis=0).astype(BF16)
        qpe = jnp.concatenate(
            [qp_ref[h // 2, rows, :][:, rope * (h % 2):rope * (h % 2 + 1)] for h in range(heads)], axis=0)
        cc = cc_ref[bi].astype(BF16)
        ckt = ckt_ref[bi].astype(BF16)
        cn = ckvn_ref[rows, :].astype(BF16)
        kn = kpen_ref[rows, :].astype(BF16)
        s_past = (_dot_nt(qlat, cc) + _dot(qpe, ckt)) * scale
        s_new = (_dot_nt(qlat, cn) + _dot_nt(qpe, kn)) * scale
        m = jnp.maximum(jnp.max(s_past, axis=-1, keepdims=True), jnp.max(s_new, axis=-1, keepdims=True))
        p_past = jnp.exp(s_past - m)
        p_new = jnp.exp(s_new - m)
        l = jnp.sum(p_past, axis=-1, keepdims=True) + jnp.sum(p_new, axis=-1, keepdims=True)
        o_lat = _dot(p_past.astype(BF16), cc) + _dot(p_new.astype(BF16), cn)
        ob = (o_lat * (1.0 / l)).astype(BF16)
        o = jnp.concatenate(
            [_dot(ob[dq * h:dq * (h + 1), :], wuv[:, LANES * h:LANES * (h + 1)]) for h in range(heads)], axis=-1)
        o_ref[rows, :] = o.astype(BF16)


def _mla_sample(qnope, qpe, ckv_new, kpe_new, cache_ckv, cache_kpe_t, layer, w, streams, dq, heads, scale):
    past, lora = cache_ckv.shape[2], cache_ckv.shape[3]
    rope = cache_kpe_t.shape[2]
    n_v = w["w_uv"].shape[2]
    nb = MLA_SAMPLE_STREAMS
    row = lambda b: (b, 0)
    lrow = lambda b: (layer, b, 0)
    body = functools.partial(_mla_sample_body, nb=nb, heads=heads, dq=dq, scale=scale)
    return pl.pallas_call(
        body, grid=(streams // nb,),
        in_specs=[pl.BlockSpec((heads, nb * dq, LANES), lambda b: (0, b, 0)),
                  pl.BlockSpec((heads // 2, nb * dq, LANES), lambda b: (0, b, 0)),
                  pl.BlockSpec((None, nb * dq, lora), lrow), pl.BlockSpec((None, nb * dq, rope), lrow),
                  pl.BlockSpec((None, nb, past, lora), lambda b: (layer, b, 0, 0)),
                  pl.BlockSpec((None, nb, rope, past), lambda b: (layer, b, 0, 0)),
                  _layer_resident(w["w_ukt"], layer), _layer_resident(w["w_uv"], layer)],
        out_specs=pl.BlockSpec((nb * dq, n_v), row),
        out_shape=jax.ShapeDtypeStruct((streams * dq, n_v), BF16),
        compiler_params=_params(1), name="mla_sample",
    )(qnope, qpe, ckv_new, kpe_new, cache_ckv, cache_kpe_t, w["w_ukt"], w["w_uv"])


def _ret_body(*refs, nb, ts, chunk, heads, dk, dv, has_init, has_prev):
    it = iter(refs)
    x_ref, g_ref, win_ref, gn_ref = next(it), next(it), next(it), next(it)
    cos_ref, sin_ref, dec_ref, xi_ref, zeta_ref, gl_ref = (next(it), next(it), next(it), next(it), next(it),
                                                           next(it))
    if has_init:
        s0_ref = next(it)
    if has_prev:
        next(it)
    y_ref, st_ref = next(it), next(it)

    @pl.when(pl.program_id(1) == 0)
    def _():
        st_ref[...] = s0_ref[...] if has_init else jnp.zeros_like(st_ref)

    d = x_ref.shape[-1]
    half = dk // 2
    k_off, v_off, g_off = heads * dk, 2 * heads * dk, 2 * heads * dk + heads * dv

    def project(x):
        return _dot(_rms(x, g_ref[...]).astype(BF16), win_ref[...])

    shared = chunk < LANES
    if shared:
        qkvg_all = project(x_ref[...].reshape(nb * ts, d))
    for bi in range(nb):
        for r0 in range(0, ts, chunk):
            if shared:
                qkvg = qkvg_all[bi * ts + r0:bi * ts + r0 + chunk, :]
            else:
                qkvg = project(x_ref[bi, r0:r0 + chunk, :])
            cos, sin = cos_ref[r0:r0 + chunk, :], sin_ref[r0:r0 + chunk, :]

            def rot(u):
                u1, u2 = u[:, :half], u[:, half:]
                return jnp.concatenate([u1 * cos - u2 * sin, u1 * sin + u2 * cos], axis=-1)

            ys = []
            for hh in range(heads):
                q = rot(qkvg[:, dk * hh:dk * (hh + 1)])
                k = rot(qkvg[:, k_off + dk * hh:k_off + dk * (hh + 1)]) * (dk ** -0.5)
                v = qkvg[:, v_off + dv * hh:v_off + dv * (hh + 1)].astype(BF16)
                gate = qkvg[:, g_off + dv * hh:g_off + dv * (hh + 1)]
                inner = (_dot_nt(q.astype(BF16), k.astype(BF16)) * dec_ref[hh]).astype(BF16)
                state = st_ref[bi, hh]
                o = _dot(inner, v) + _dot((q * xi_ref[hh]).astype(BF16), state.astype(BF16))
                st_ref[bi, hh] = gl_ref[hh] * state + _dot_tn((k * zeta_ref[hh]).astype(BF16), v)
                ys.append((_silu(gate) * _rms(o, gn_ref[hh])).astype(BF16))
            y_ref[bi, r0:r0 + chunk, :] = jnp.concatenate(ys, axis=-1)


def _ret(x, layer, mix_layer, w, tables, state0, prev, nb, ts):
    chunk = tables[2].shape[1]
    batch, seq, d = x.shape
    heads, dk, dv = w["heads"], w["dk"], w["dv"]
    n_layers = w["w_in"].shape[0]
    cos, sin, dec, xi, zeta, gl = tables
    st_blk = pl.BlockSpec((None, nb, heads, dk, dv), lambda b, s: (layer, b, 0, 0, 0))
    args = [x, w["norm"], w["w_in"], w["gn"], cos, sin, dec, xi, zeta, gl]
    specs = [pl.BlockSpec((nb, ts, d), lambda b, s: (b, s, 0)), _layer_resident(w["norm"], mix_layer),
             _layer_resident(w["w_in"], layer), _layer_resident(w["gn"], layer),
             pl.BlockSpec((ts, LANES), lambda b, s: (s, 0)), pl.BlockSpec((ts, LANES), lambda b, s: (s, 0)),
             _resident(dec.shape), _resident(xi.shape), _resident(zeta.shape), _resident(gl.shape)]
    if state0 is not None:
        args.append(state0)
        specs.append(st_blk)
    aliases = {}
    if prev is not None:
        aliases = {len(args): 1}
        args.append(prev)
        specs.append(pl.BlockSpec(memory_space=pl.ANY))
    body = functools.partial(_ret_body, nb=nb, ts=ts, chunk=chunk, heads=heads, dk=dk, dv=dv,
                             has_init=state0 is not None, has_prev=prev is not None)
    return pl.pallas_call(
        body, grid=(batch // nb, seq // ts), in_specs=specs,
        out_specs=[pl.BlockSpec((nb, ts, heads * dv), lambda b, s: (b, s, 0)), st_blk],
        out_shape=[jax.ShapeDtypeStruct((batch, seq, heads * dv), BF16),
                   jax.ShapeDtypeStruct((n_layers, batch, heads, dk, dv), F32)],
        input_output_aliases=aliases,
        compiler_params=_params(2), name="retention",
    )(*args)


def _rope_tables(pos, dim, width):
    inv = 1.0 / (ROPE_THETA ** (jnp.arange(0, dim, 2, dtype=F32) / dim))
    ang = pos.astype(F32)[:, None] * inv[None, :]
    reps = width // (dim // 2)
    return jnp.tile(jnp.cos(ang), (1, reps)), jnp.tile(jnp.sin(ang), (1, reps))


def _ret_tables(pos, length, heads, dk):
    cos, sin = _rope_tables(pos, dk, dk // 2)
    lg = jnp.log(1.0 - 2.0 ** (-5.0 - jnp.arange(heads, dtype=F32)))
    idx = jnp.arange(length, dtype=F32)
    diff = idx[:, None] - idx[None, :]
    dec = jnp.where(diff >= 0, jnp.exp(lg[:, None, None] * jnp.maximum(diff, 0.0)), 0.0)
    xi = jnp.exp(lg[:, None] * (idx[None, :] + 1.0))[:, :, None]
    zeta = jnp.exp(lg[:, None] * (length - 1.0 - idx[None, :]))[:, :, None]
    gl = jnp.exp(lg * length)[:, None, None]
    return cos, sin, dec, xi, zeta, gl


def _mla_weights(norm, w_dq, q_norm, w_uq, w_dkv, kv_norm, w_uk, w_uv, w_o):
    n, lora, heads, nope = w_uk.shape
    q_lora = w_dq.shape[2]
    rope = w_dkv.shape[2] - lora
    wq = w_uq.reshape(n, q_lora, heads, nope + rope)
    w_uq_cat = jnp.concatenate([wq[..., :nope].reshape(n, q_lora, heads * nope),
                                wq[..., nope:].reshape(n, q_lora, heads * rope)], axis=-1)
    wk_pe = w_dkv[..., lora:]
    w_dkv_cat = jnp.concatenate([w_dkv[..., :lora], wk_pe, wk_pe], axis=-1)
    return {
        "norm": norm[:, None, :], "w_dq": w_dq.astype(BF16), "q_norm": q_norm[:, None, :],
        "w_uq": w_uq_cat.astype(BF16), "w_dkv": w_dkv_cat.astype(BF16), "kv_norm": kv_norm[:, None, :],
        "w_ukt": jnp.swapaxes(w_uk.reshape(n, lora, heads * nope), 1, 2).astype(BF16),
        "w_uv": w_uv.reshape(n, lora, -1).astype(BF16), "w_o": w_o.astype(BF16),
        "n_nope": heads * nope, "n_pe": heads * rope, "heads": heads,
        "scale": float(nope + rope) ** -0.5,
    }


def kernel(x_prompt, x_sample, cache_ckv, cache_kpe, state_ret, norm_ffn1, w_ffn1_gate, w_ffn1_up, w_ffn1_down, norm_mix, w_dq, q_norm, w_uq, w_dkv, kv_norm, w_uk, w_uv, w_o_mla, w_ret_in, ret_gn, w_ret_out, norm_ffn2, w_ffn2_gate, w_ffn2_up, w_ffn2_down, norm_final):
    batch, seq, d = x_prompt.shape
    streams, dseq, _ = x_sample.shape
    past = cache_ckv.shape[2]
    depth = norm_ffn1.shape[0]
    n_ret, _, ret_heads, dk, dv = state_ret.shape
    rope = cache_kpe.shape[3]
    assert rope == LANES // 2 and w_uk.shape[3] == LANES and w_uv.shape[3] == LANES
    assert seq % ATTN_Q_TILE == 0 and seq % RET_STEP == 0 and RET_STEP % RET_TILE == 0 and seq % TOKEN_TILE_MLA == 0
    assert streams % RET_SAMPLE_STREAMS == 0 and streams % MLA_SAMPLE_STREAMS == 0

    pos_p = jnp.arange(seq)
    pos_s = past + jnp.arange(dseq)
    mla_cos_p, mla_sin_p = _rope_tables(pos_p, rope, LANES)
    mla_cos_s, mla_sin_s = _rope_tables(jnp.tile(pos_s, streams), rope, LANES)
    ret_tab_p = _ret_tables(pos_p, RET_TILE, ret_heads, dk)
    ret_tab_s = _ret_tables(pos_s, dseq, ret_heads, dk)

    ffn1 = (norm_ffn1[:, None, :], w_ffn1_gate.astype(BF16), w_ffn1_up.astype(BF16), w_ffn1_down.astype(BF16))
    ffn2 = (norm_ffn2[:, None, :], w_ffn2_gate.astype(BF16), w_ffn2_up.astype(BF16), w_ffn2_down.astype(BF16))
    mla = _mla_weights(norm_mix[0::2], w_dq, q_norm, w_uq, w_dkv, kv_norm, w_uk, w_uv, w_o_mla)
    ret = {"norm": norm_mix[:, None, :], "w_in": w_ret_in.astype(BF16),
           "gn": ret_gn.reshape(n_ret, ret_heads, 1, dv), "heads": ret_heads, "dk": dk, "dv": dv}
    w_ret_out_b = w_ret_out.astype(BF16)
    final_g = norm_final[None, :]
    cache_kpe_t = jnp.swapaxes(cache_kpe, 2, 3)

    xp = x_prompt.reshape(batch * seq, d)
    xs = x_sample.reshape(streams * dseq, d)
    lat_p = lat_s = st_p = st_s = None
    for i in range(depth):
        j = i // 2
        if i % 2 == 0:
            heads, scale = mla["heads"], mla["scale"]
            xp = _ffn(xp, i, *ffn1)
            qn, qp, c_p, r_p, knt, v, kpt = _mla_proj(xp, (j, mla, mla_cos_p, mla_sin_p, lat_p, seq))
            lat_p = (c_p, r_p)
            o_p = _mla_attn(qn, qp, knt, kpt, v, batch, seq, scale)
            xs, qn, qp, c_s, r_s, _, _, _ = _ffn(xs, i, *ffn1, mla=(j, mla, mla_cos_s, mla_sin_s, lat_s, None))
            lat_s = (c_s, r_s)
            o_s = _mla_sample(qn, qp, c_s, r_s, cache_ckv, cache_kpe_t, j, mla, streams, dseq, heads, scale)
            out_proj = (mla["w_o"], j)
        else:
            xp = _ffn(xp, i, *ffn1)
            xs = _ffn(xs, i, *ffn1)
            o_p, st_p = _ret(xp.reshape(batch, seq, d), j, i, ret, ret_tab_p, None, st_p, 1, RET_STEP)
            o_s, st_s = _ret(xs.reshape(streams, dseq, d), j, i, ret, ret_tab_s, state_ret, st_s,
                             RET_SAMPLE_STREAMS, dseq)
            o_p = o_p.reshape(batch * seq, -1)
            o_s = o_s.reshape(streams * dseq, -1)
            out_proj = (w_ret_out_b, j)
        fg = final_g if i == depth - 1 else None
        xp = _ffn(xp, i, *ffn2, proj=(o_p,) + out_proj, final_g=fg)
        xs = _ffn(xs, i, *ffn2, proj=(o_s,) + out_proj, final_g=fg)
    n_mla = lat_p[0].shape[0]
    return (xp.reshape(batch, seq, d), xs.reshape(streams, dseq, d),
            lat_p[0].reshape(n_mla, batch, seq, -1), jnp.swapaxes(lat_p[1], 2, 3), st_p,
            lat_s[0].reshape(n_mla, streams, dseq, -1), lat_s[1].reshape(n_mla, streams, dseq, -1), st_s)
```

```python
import functools

import jax
import jax.numpy as jnp
from jax import lax
from jax.experimental import pallas as pl
from jax.experimental.pallas import tpu as pltpu

F32 = jnp.float32
BF16 = jnp.bfloat16

EPS = 1e-6
ROPE_THETA = 10000.0
LOG2_E = 1.4426950408889634
CHUNK = 64
LANES = 128
V7X_VMEM_LIMIT_BYTES = 56 * 1024 * 1024

TOKEN_TILE = 1024
TOKEN_TILE_MLA = 512
FFN_GROUP_ROWS = 256
FFN_STREAM_CHUNK = 256
ATTN_Q_TILE = 512
ATTN_HEAD_GROUP = 8
RET_TILE = 256
RET_STEP = 512
RET_SAMPLE_STREAMS = 4
MLA_SAMPLE_STREAMS = 2


def _params(n_axes):
    return pltpu.CompilerParams(dimension_semantics=("arbitrary",) * n_axes,
                                vmem_limit_bytes=V7X_VMEM_LIMIT_BYTES)


def _resident(shape):
    zeros = (0,) * len(shape)
    return pl.BlockSpec(shape, lambda *_: zeros, pipeline_mode=pl.Buffered(1))


def _layer_resident(arr, layer):
    idx = (layer,) + (0,) * (arr.ndim - 1)
    return pl.BlockSpec((None,) + arr.shape[1:], lambda *_: idx, pipeline_mode=pl.Buffered(1))


def _dot(a, b):
    return jnp.dot(a, b, preferred_element_type=F32)


def _dot_nt(a, b):
    return lax.dot_general(a, b, (((1,), (1,)), ((), ())), preferred_element_type=F32)


def _dot_tn(a, b):
    return lax.dot_general(a, b, (((0,), (0,)), ((), ())), preferred_element_type=F32)


def _rms(x, g):
    ms = jnp.mean(x * x, axis=-1, keepdims=True)
    return x * lax.rsqrt(ms + EPS) * g


def _silu(x):
    return x * (1.0 / (1.0 + jnp.exp(-x)))


def _ffn_body(*refs, has_proj, has_final, mla):
    it = iter(refs)
    x_ref = next(it)
    if has_proj:
        o_ref, wp_ref = next(it), next(it)
    g_ref, wg_ref, wu_ref, wd_ref = next(it), next(it), next(it), next(it)
    if has_final:
        gf_ref = next(it)
    if mla is not None:
        mla_w = [next(it) for _ in MLA_WEIGHT_NAMES]
        cos_ref, sin_ref = next(it), next(it)
        if mla["has_prev"]:
            next(it), next(it)
    out_ref = next(it)
    mla_out = [next(it) for _ in range(N_MLA_OUTPUTS)] if mla is not None else None

    sub = min(FFN_GROUP_ROWS, x_ref.shape[0])
    if has_proj:
        if len(o_ref.shape) == 3:
            o = jnp.concatenate([o_ref[hh] for hh in range(o_ref.shape[0])], axis=-1)
        else:
            o = o_ref[...]
        mixed = _dot(o, wp_ref[...])
    results = []
    for si in range(x_ref.shape[0] // sub):
        rows = slice(si * sub, (si + 1) * sub)
        x = x_ref[rows, :]
        if has_proj:
            x = x + mixed[rows, :]
        h = _rms(x, g_ref[...]).astype(BF16)
        gate = _dot(h, wg_ref[...])
        up = _dot(h, wu_ref[...])
        a = (_silu(gate) * up).astype(BF16)
        r = x + 0.5 * _dot(a, wd_ref[...])
        if has_final:
            r = _rms(r, gf_ref[...])
        out_ref[rows, :] = r
        results.append(r)
    if mla is not None:
        tile = results[0] if len(results) == 1 else jnp.concatenate(results, axis=0)
        _mla_project(tile, mla_w, cos_ref, sin_ref, mla_out, n_nope=mla["n_nope"], n_pe=mla["n_pe"],
                     lora=mla["lora"], kpe_by_feature=mla["kpe_by_feature"])


def _ffn_stream_body(*refs, has_proj, has_final):
    it = iter(refs)
    x_ref = next(it)
    if has_proj:
        o_ref, wp_ref = next(it), next(it)
    g_ref, wg_ref, wu_ref, wd_ref = next(it), next(it), next(it), next(it)
    if has_final:
        gf_ref = next(it)
    out_ref, x_sc, h_sc, acc_sc = next(it), next(it), next(it), next(it)
    k = pl.program_id(0)

    @pl.when(k == 0)
    def _():
        x = x_ref[...]
        if has_proj:
            x = x + _dot(o_ref[...], wp_ref[...])
        x_sc[...] = x
        h_sc[...] = _rms(x, g_ref[...]).astype(BF16)
        acc_sc[...] = jnp.zeros_like(acc_sc)

    h = h_sc[...]
    a = (_silu(_dot(h, wg_ref[...])) * _dot(h, wu_ref[...])).astype(BF16)
    acc_sc[...] += _dot(a, wd_ref[...])

    @pl.when(k == pl.num_programs(0) - 1)
    def _():
        r = x_sc[...] + 0.5 * acc_sc[...]
        if has_final:
            r = _rms(r, gf_ref[...])
        out_ref[...] = r


def _ffn_stream(x, layer, g, wg, wu, wd, proj=None, final_g=None):
    t, d = x.shape
    f = wg.shape[2]
    fc = FFN_STREAM_CHUNK
    assert f % fc == 0
    whole = lambda k: (0, 0)
    args, specs = [x], [pl.BlockSpec((t, d), whole)]
    if proj is not None:
        o, wp, wp_layer = proj
        args += [o, wp]
        specs += [pl.BlockSpec(o.shape, whole), _layer_resident(wp, wp_layer)]
    args += [g, wg, wu, wd]
    specs += [_layer_resident(g, layer),
              pl.BlockSpec((None, d, fc), lambda k: (layer, 0, k)), pl.BlockSpec((None, d, fc), lambda k: (layer, 0, k)),
              pl.BlockSpec((None, fc, d), lambda k: (layer, k, 0))]
    if final_g is not None:
        args.append(final_g)
        specs.append(_resident(final_g.shape))
    body = functools.partial(_ffn_stream_body, has_proj=proj is not None, has_final=final_g is not None)
    return pl.pallas_call(
        body, grid=(f // fc,), in_specs=specs,
        out_specs=pl.BlockSpec((t, d), whole),
        out_shape=jax.ShapeDtypeStruct((t, d), F32),
        scratch_shapes=[pltpu.VMEM((t, d), F32), pltpu.VMEM((t, d), BF16), pltpu.VMEM((t, d), F32)],
        compiler_params=_params(1), name="ffn_stream",
    )(*args)


def _ffn(x, layer, g, wg, wu, wd, proj=None, final_g=None, mla=None):
    t, d = x.shape
    tm = min(TOKEN_TILE if mla is None else TOKEN_TILE_MLA, t)
    assert t % tm == 0
    row = lambda i: (i, 0)
    args, specs = [x], [pl.BlockSpec((tm, d), row)]
    if proj is not None:
        o, wp, wp_layer = proj
        o_spec = (pl.BlockSpec((tm, o.shape[1]), row) if o.ndim == 2
                  else pl.BlockSpec((o.shape[0], tm, o.shape[2]), lambda i: (0, i, 0)))
        args += [o, wp]
        specs += [o_spec, _layer_resident(wp, wp_layer)]
    args += [g, wg, wu, wd]
    specs += [_layer_resident(a, layer) for a in (g, wg, wu, wd)]
    if final_g is not None:
        args.append(final_g)
        specs.append(_resident(final_g.shape))
    out_specs = [pl.BlockSpec((tm, d), row)]
    out_shape = [jax.ShapeDtypeStruct((t, d), F32)]
    aliases, mla_static = {}, None
    if mla is not None:
        aliases, mla_static = _mla_plumbing(mla, t, tm, args, specs, out_specs, out_shape)
    body = functools.partial(_ffn_body, has_proj=proj is not None, has_final=final_g is not None, mla=mla_static)
    outs = pl.pallas_call(
        body, grid=(t // tm,), in_specs=specs, out_specs=out_specs, out_shape=out_shape,
        input_output_aliases=aliases,
        compiler_params=_params(1), name="ffn_half",
    )(*args)
    return outs[0] if mla is None else outs


MLA_WEIGHT_NAMES = ("norm", "w_dq", "q_norm", "w_uq", "w_dkv", "kv_norm", "w_ukt", "w_uv")
N_MLA_OUTPUTS = 7


def _mla_project(x, w_refs, cos_ref, sin_ref, out_refs, *, n_nope, n_pe, lora, kpe_by_feature):
    g_ref, wdq_ref, qn_ref, wuq_ref, wdkv_ref, kvn_ref, wukt_ref, wuv_ref = w_refs
    qnope_ref, qpe_ref, ckv_ref, kpe_ref, knt_ref, v_ref, kpt_ref = out_refs
    rope = LANES // 2
    h = _rms(x, g_ref[...]).astype(BF16)
    cq = _rms(_dot(h, wdq_ref[...]), qn_ref[...]).astype(BF16)
    q = _dot(cq, wuq_ref[...])
    cos, sin = cos_ref[...], sin_ref[...]
    first_half = lax.broadcasted_iota(jnp.int32, cos.shape, 1) % rope < rope // 2

    def rotary(u):
        below = pltpu.roll(u, rope // 2, 1)
        above = pltpu.roll(u, LANES - rope // 2, 1)
        return u * cos + jnp.where(first_half, -above, below) * sin

    for hh in range(n_nope // LANES):
        qnope_ref[hh] = q[:, hh * LANES:(hh + 1) * LANES].astype(BF16)
    for pp in range(n_pe // LANES):
        qpe_ref[pp] = rotary(q[:, n_nope + pp * LANES:n_nope + (pp + 1) * LANES]).astype(BF16)
    kv = _dot(h, wdkv_ref[...])
    ckv = _rms(kv[:, :lora], kvn_ref[...])
    ckv_ref[...] = ckv
    kpe2 = rotary(kv[:, lora:])
    kpe2_t = kpe2.T
    if kpe_by_feature:
        kpe_ref[...] = kpe2_t[:rope, :]
    else:
        kpe_ref[...] = kpe2[:, :rope]
    kpt_ref[...] = kpe2_t.astype(BF16)
    cb = ckv.astype(BF16)
    k_nope_t = _dot_nt(wukt_ref[...], cb).astype(BF16)
    v = _dot(cb, wuv_ref[...]).astype(BF16)
    for hh in range(n_nope // LANES):
        knt_ref[hh] = k_nope_t[hh * LANES:(hh + 1) * LANES, :]
        v_ref[hh] = v[:, hh * LANES:(hh + 1) * LANES]


def _mla_plumbing(mla, t, tm, args, specs, out_specs, out_shape):
    m_layer, w, cos, sin, prev, kpe_seq = mla
    n_layers = w["w_dq"].shape[0]
    n_pos_tiles = cos.shape[0] // tm
    n_nope, n_pe = w["n_nope"], w["n_pe"]
    lora, rope = w["kv_norm"].shape[2], LANES // 2
    assert w["w_uv"].shape[2] == n_nope
    col = lambda i: (0, i)
    lrow = lambda i: (m_layer, i, 0)
    pos = lambda i: (i % n_pos_tiles, 0)
    args += [w[n] for n in MLA_WEIGHT_NAMES] + [cos, sin]
    specs += ([_layer_resident(w[n], m_layer) for n in MLA_WEIGHT_NAMES]
              + [pl.BlockSpec((tm, LANES), pos), pl.BlockSpec((tm, LANES), pos)])
    aliases = {}
    if prev is not None:
        first = len(out_specs)
        aliases = {len(args): first + 2, len(args) + 1: first + 3}
        args += list(prev)
        specs += [pl.BlockSpec(memory_space=pl.ANY)] * 2
    if kpe_seq is not None:
        per_seq = kpe_seq // tm
        kpe_spec = pl.BlockSpec((None, None, rope, tm), lambda i: (m_layer, i // per_seq, 0, i % per_seq))
        kpe_shape = jax.ShapeDtypeStruct((n_layers, t // kpe_seq, rope, kpe_seq), F32)
    else:
        kpe_spec = pl.BlockSpec((None, tm, rope), lrow)
        kpe_shape = jax.ShapeDtypeStruct((n_layers, t, rope), F32)
    heads, pairs = n_nope // LANES, n_pe // LANES
    hrow = lambda i: (0, i, 0)
    hcol = lambda i: (0, 0, i)
    out_specs += [pl.BlockSpec((heads, tm, LANES), hrow), pl.BlockSpec((pairs, tm, LANES), hrow),
                  pl.BlockSpec((None, tm, lora), lrow), kpe_spec,
                  pl.BlockSpec((heads, LANES, tm), hcol), pl.BlockSpec((heads, tm, LANES), hrow),
                  pl.BlockSpec((LANES, tm), col)]
    out_shape += [jax.ShapeDtypeStruct((heads, t, LANES), BF16), jax.ShapeDtypeStruct((pairs, t, LANES), BF16),
                  jax.ShapeDtypeStruct((n_layers, t, lora), F32), kpe_shape,
                  jax.ShapeDtypeStruct((heads, LANES, t), BF16), jax.ShapeDtypeStruct((heads, t, LANES), BF16),
                  jax.ShapeDtypeStruct((LANES, t), BF16)]
    return aliases, {"n_nope": n_nope, "n_pe": n_pe, "lora": lora, "has_prev": prev is not None,
                     "kpe_by_feature": kpe_seq is not None}


def _mla_proj_body(x_ref, *refs, mla):
    n_w = len(MLA_WEIGHT_NAMES)
    _mla_project(x_ref[...], refs[:n_w], refs[n_w], refs[n_w + 1], refs[n_w + 2 + 2 * mla["has_prev"]:],
                 n_nope=mla["n_nope"], n_pe=mla["n_pe"], lora=mla["lora"], kpe_by_feature=mla["kpe_by_feature"])


def _mla_proj(x, mla):
    t, d = x.shape
    tm = min(TOKEN_TILE, t)
    assert t % tm == 0
    args, specs, out_specs, out_shape = [x], [pl.BlockSpec((tm, d), lambda i: (i, 0))], [], []
    aliases, static = _mla_plumbing(mla, t, tm, args, specs, out_specs, out_shape)
    return pl.pallas_call(
        functools.partial(_mla_proj_body, mla=static), grid=(t // tm,), in_specs=specs,
        out_specs=out_specs, out_shape=out_shape, input_output_aliases=aliases,
        compiler_params=_params(1), name="mla_proj",
    )(*args)


def _mla_attn_body(qn_ref, qp_ref, knt_ref, kpt_ref, v_ref, o_ref, kcat_ref, vext_ref, *, seq, tq, group, exp2_scale):
    lane = lax.broadcasted_iota(jnp.int32, (tq, LANES), 1)
    qc = lax.broadcasted_iota(jnp.int32, (tq, tq), 0) // CHUNK
    kc = lax.broadcasted_iota(jnp.int32, (tq, tq), 1) // CHUNK
    visible = kc <= qc
    for slot in range(2):
        kcat_ref[slot, LANES:, :] = kpt_ref[...]
        vext_ref[slot, :, LANES:] = jnp.ones((seq, LANES), BF16)

    def head_pair(pp, carry):
        for slot in range(2):
            hh = 2 * pp + slot
            keep = (lane >= LANES // 2) == bool(slot)
            kcat_ref[slot, :LANES, :] = knt_ref[hh]
            vext_ref[slot, :, :LANES] = v_ref[hh]
            for qi in reversed(range(seq // tq)):
                q0 = qi * tq
                qp = jnp.where(keep, qp_ref[pp, q0:q0 + tq, :].astype(F32), 0.0).astype(BF16)
                qh = jnp.concatenate([qn_ref[hh, q0:q0 + tq, :], qp], axis=-1)
                s_d = jnp.where(visible, _dot(qh, kcat_ref[slot, :, q0:q0 + tq]), -jnp.inf)
                m = jnp.max(s_d, axis=-1, keepdims=True)
                if qi > 0:
                    s_p = _dot(qh, kcat_ref[slot, :, 0:q0])
                    m = jnp.maximum(m, jnp.max(s_p, axis=-1, keepdims=True))
                acc = _dot(jnp.exp2((s_d - m) * exp2_scale).astype(BF16), vext_ref[slot, q0:q0 + tq, :])
                if qi > 0:
                    acc = acc + _dot(jnp.exp2((s_p - m) * exp2_scale).astype(BF16), vext_ref[slot, 0:q0, :])
                o_ref[hh, q0:q0 + tq, :] = (acc[:, :LANES] * (1.0 / acc[:, LANES:])).astype(BF16)
        return carry

    lax.fori_loop(0, group // 2, head_pair, 0)


def _mla_attn(qnope, qpe, knt, kpt, v, batch, seq, scale):
    heads, t, _ = qnope.shape
    tq = min(ATTN_Q_TILE, seq)
    group = min(ATTN_HEAD_GROUP, heads)
    assert heads % group == 0 and group % 2 == 0
    body = functools.partial(_mla_attn_body, seq=seq, tq=tq, group=group, exp2_scale=scale * LOG2_E)
    blk = pl.BlockSpec((group, seq, LANES), lambda b, g: (g, b, 0))
    return pl.pallas_call(
        body, grid=(batch, heads // group),
        in_specs=[blk, pl.BlockSpec((group // 2, seq, LANES), lambda b, g: (g, b, 0)),
                  pl.BlockSpec((group, LANES, seq), lambda b, g: (g, 0, b)),
                  pl.BlockSpec((LANES, seq), lambda b, g: (0, b)), blk],
        out_specs=blk,
        out_shape=jax.ShapeDtypeStruct((heads, t, LANES), BF16),
        scratch_shapes=[pltpu.VMEM((2, 2 * LANES, seq), BF16), pltpu.VMEM((2, seq, 2 * LANES), BF16)],
        compiler_params=_params(2), name="mla_attn",
    )(qnope, qpe, knt, kpt, v)


def _mla_sample_body(qn_ref, qp_ref, ckvn_ref, kpen_ref, cc_ref, ckt_ref, wukt_ref, wuv_ref, o_ref,
                     *, nb, heads, dq, scale):
    wukt, wuv = wukt_ref[...], wuv_ref[...]
    rope = LANES // 2
    for bi in range(nb):
        rows = slice(bi * dq, (bi + 1) * dq)
        qlat = jnp.concatenate(
            [_dot(qn_ref[h, rows, :], wukt[LANES * h:LANES * (h + 1), :]) for h in range(heads)],
            axis=0).astype(BF16)
        qpe = jnp.concatenate(
            [qp_ref[h // 2, rows, :][:, rope * (h % 2):rope * (h % 2 + 1)] for h in range(heads)], axis=0)
        cc = cc_ref[bi].astype(BF16)
        ckt = ckt_ref[bi].astype(BF16)
        cn = ckvn_ref[rows, :].astype(BF16)
        kn = kpen_ref[rows, :].astype(BF16)
        s_past = (_dot_nt(qlat, cc) + _dot(qpe, ckt)) * scale
        s_new = (_dot_nt(qlat, cn) + _dot_nt(qpe, kn)) * scale
        m = jnp.maximum(jnp.max(s_past, axis=-1, keepdims=True), jnp.max(s_new, axis=-1, keepdims=True))
        p_past = jnp.exp(s_past - m)
        p_new = jnp.exp(s_new - m)
        l = jnp.sum(p_past, axis=-1, keepdims=True) + jnp.sum(p_new, axis=-1, keepdims=True)
        o_lat = _dot(p_past.astype(BF16), cc) + _dot(p_new.astype(BF16), cn)
        ob = (o_lat * (1.0 / l)).astype(BF16)
        o = jnp.concatenate(
            [_dot(ob[dq * h:dq * (h + 1), :], wuv[:, LANES * h:LANES * (h + 1)]) for h in range(heads)], axis=-1)
        o_ref[rows, :] = o.astype(BF16)


def _mla_sample(qnope, qpe, ckv_new, kpe_new, cache_ckv, cache_kpe_t, layer, w, streams, dq, heads, scale):
    past, lora = cache_ckv.shape[2], cache_ckv.shape[3]
    rope = cache_kpe_t.shape[2]
    n_v = w["w_uv"].shape[2]
    nb = MLA_SAMPLE_STREAMS
    row = lambda b: (b, 0)
    lrow = lambda b: (layer, b, 0)
    body = functools.partial(_mla_sample_body, nb=nb, heads=heads, dq=dq, scale=scale)
    return pl.pallas_call(
        body, grid=(streams // nb,),
        in_specs=[pl.BlockSpec((heads, nb * dq, LANES), lambda b: (0, b, 0)),
                  pl.BlockSpec((heads // 2, nb * dq, LANES), lambda b: (0, b, 0)),
                  pl.BlockSpec((None, nb * dq, lora), lrow), pl.BlockSpec((None, nb * dq, rope), lrow),
                  pl.BlockSpec((None, nb, past, lora), lambda b: (layer, b, 0, 0)),
                  pl.BlockSpec((None, nb, rope, past), lambda b: (layer, b, 0, 0)),
                  _layer_resident(w["w_ukt"], layer), _layer_resident(w["w_uv"], layer)],
        out_specs=pl.BlockSpec((nb * dq, n_v), row),
        out_shape=jax.ShapeDtypeStruct((streams * dq, n_v), BF16),
        compiler_params=_params(1), name="mla_sample",
    )(qnope, qpe, ckv_new, kpe_new, cache_ckv, cache_kpe_t, w["w_ukt"], w["w_uv"])


def _ret_body(*refs, nb, ts, chunk, heads, dk, dv, has_init, has_prev):
    it = iter(refs)
    x_ref, g_ref, win_ref, gn_ref = next(it), next(it), next(it), next(it)
    cos_ref, sin_ref, dec_ref, xi_ref, zeta_ref, gl_ref = (next(it), next(it), next(it), next(it), next(it),
                                                           next(it))
    if has_init:
        s0_ref = next(it)
    if has_prev:
        next(it)
    y_ref, st_ref = next(it), next(it)

    @pl.when(pl.program_id(1) == 0)
    def _():
        st_ref[...] = s0_ref[...] if has_init else jnp.zeros_like(st_ref)

    d = x_ref.shape[-1]
    half = dk // 2
    k_off, v_off, g_off = heads * dk, 2 * heads * dk, 2 * heads * dk + heads * dv

    def project(x):
        return _dot(_rms(x, g_ref[...]).astype(BF16), win_ref[...])

    shared = chunk < LANES
    if shared:
        qkvg_all = project(x_ref[...].reshape(nb * ts, d))
    for bi in range(nb):
        for r0 in range(0, ts, chunk):
            if shared:
                qkvg = qkvg_all[bi * ts + r0:bi * ts + r0 + chunk, :]
            else:
                qkvg = project(x_ref[bi, r0:r0 + chunk, :])
            cos, sin = cos_ref[r0:r0 + chunk, :], sin_ref[r0:r0 + chunk, :]

            def rot(u):
                u1, u2 = u[:, :half], u[:, half:]
                return jnp.concatenate([u1 * cos - u2 * sin, u1 * sin + u2 * cos], axis=-1)

            ys = []
            for hh in range(heads):
                q = rot(qkvg[:, dk * hh:dk * (hh + 1)])
                k = rot(qkvg[:, k_off + dk * hh:k_off + dk * (hh + 1)]) * (dk ** -0.5)
                v = qkvg[:, v_off + dv * hh:v_off + dv * (hh + 1)].astype(BF16)
                gate = qkvg[:, g_off + dv * hh:g_off + dv * (hh + 1)]
                inner = (_dot_nt(q.astype(BF16), k.astype(BF16)) * dec_ref[hh]).astype(BF16)
                state = st_ref[bi, hh]
                o = _dot(inner, v) + _dot((q * xi_ref[hh]).astype(BF16), state.astype(BF16))
                st_ref[bi, hh] = gl_ref[hh] * state + _dot_tn((k * zeta_ref[hh]).astype(BF16), v)
                ys.append((_silu(gate) * _rms(o, gn_ref[hh])).astype(BF16))
            y_ref[bi, r0:r0 + chunk, :] = jnp.concatenate(ys, axis=-1)


def _ret(x, layer, mix_layer, w, tables, state0, prev, nb, ts):
    chunk = tables[2].shape[1]
    batch, seq, d = x.shape
    heads, dk, dv = w["heads"], w["dk"], w["dv"]
    n_layers = w["w_in"].shape[0]
    cos, sin, dec, xi, zeta, gl = tables
    st_blk = pl.BlockSpec((None, nb, heads, dk, dv), lambda b, s: (layer, b, 0, 0, 0))
    args = [x, w["norm"], w["w_in"], w["gn"], cos, sin, dec, xi, zeta, gl]
    specs = [pl.BlockSpec((nb, ts, d), lambda b, s: (b, s, 0)), _layer_resident(w["norm"], mix_layer),
             _layer_resident(w["w_in"], layer), _layer_resident(w["gn"], layer),
             pl.BlockSpec((ts, LANES), lambda b, s: (s, 0)), pl.BlockSpec((ts, LANES), lambda b, s: (s, 0)),
             _resident(dec.shape), _resident(xi.shape), _resident(zeta.shape), _resident(gl.shape)]
    if state0 is not None:
        args.append(state0)
        specs.append(st_blk)
    aliases = {}
    if prev is not None:
        aliases = {len(args): 1}
        args.append(prev)
        specs.append(pl.BlockSpec(memory_space=pl.ANY))
    body = functools.partial(_ret_body, nb=nb, ts=ts, chunk=chunk, heads=heads, dk=dk, dv=dv,
                             has_init=state0 is not None, has_prev=prev is not None)
    return pl.pallas_call(
        body, grid=(batch // nb, seq // ts), in_specs=specs,
        out_specs=[pl.BlockSpec((nb, ts, heads * dv), lambda b, s: (b, s, 0)), st_blk],
        out_shape=[jax.ShapeDtypeStruct((batch, seq, heads * dv), BF16),
                   jax.ShapeDtypeStruct((n_layers, batch, heads, dk, dv), F32)],
        input_output_aliases=aliases,
        compiler_params=_params(2), name="retention",
    )(*args)


def _rope_tables(pos, dim, width):
    inv = 1.0 / (ROPE_THETA ** (jnp.arange(0, dim, 2, dtype=F32) / dim))
    ang = pos.astype(F32)[:, None] * inv[None, :]
    reps = width // (dim // 2)
    return jnp.tile(jnp.cos(ang), (1, reps)), jnp.tile(jnp.sin(ang), (1, reps))


def _ret_tables(pos, length, heads, dk):
    cos, sin = _rope_tables(pos, dk, dk // 2)
    lg = jnp.log(1.0 - 2.0 ** (-5.0 - jnp.arange(heads, dtype=F32)))
    idx = jnp.arange(length, dtype=F32)
    diff = idx[:, None] - idx[None, :]
    dec = jnp.where(diff >= 0, jnp.exp(lg[:, None, None] * jnp.maximum(diff, 0.0)), 0.0)
    xi = jnp.exp(lg[:, None] * (idx[None, :] + 1.0))[:, :, None]
    zeta = jnp.exp(lg[:, None] * (length - 1.0 - idx[None, :]))[:, :, None]
    gl = jnp.exp(lg * length)[:, None, None]
    return cos, sin, dec, xi, zeta, gl


def _mla_weights(norm, w_dq, q_norm, w_uq, w_dkv, kv_norm, w_uk, w_uv, w_o):
    n, lora, heads, nope = w_uk.shape
    q_lora = w_dq.shape[2]
    rope = w_dkv.shape[2] - lora
    wq = w_uq.reshape(n, q_lora, heads, nope + rope)
    w_uq_cat = jnp.concatenate([wq[..., :nope].reshape(n, q_lora, heads * nope),
                                wq[..., nope:].reshape(n, q_lora, heads * rope)], axis=-1)
    wk_pe = w_dkv[..., lora:]
    w_dkv_cat = jnp.concatenate([w_dkv[..., :lora], wk_pe, wk_pe], axis=-1)
    return {
        "norm": norm[:, None, :], "w_dq": w_dq.astype(BF16), "q_norm": q_norm[:, None, :],
        "w_uq": w_uq_cat.astype(BF16), "w_dkv": w_dkv_cat.astype(BF16), "kv_norm": kv_norm[:, None, :],
        "w_ukt": jnp.swapaxes(w_uk.reshape(n, lora, heads * nope), 1, 2).astype(BF16),
        "w_uv": w_uv.reshape(n, lora, -1).astype(BF16), "w_o": w_o.astype(BF16),
        "n_nope": heads * nope, "n_pe": heads * rope, "heads": heads,
        "scale": float(nope + rope) ** -0.5,
    }


def kernel(x_prompt, x_sample, cache_ckv, cache_kpe, state_ret, norm_ffn1, w_ffn1_gate, w_ffn1_up, w_ffn1_down, norm_mix, w_dq, q_norm, w_uq, w_dkv, kv_norm, w_uk, w_uv, w_o_mla, w_ret_in, ret_gn, w_ret_out, norm_ffn2, w_ffn2_gate, w_ffn2_up, w_ffn2_down, norm_final):
    batch, seq, d = x_prompt.shape
    streams, dseq, _ = x_sample.shape
    past = cache_ckv.shape[2]
    depth = norm_ffn1.shape[0]
    n_ret, _, ret_heads, dk, dv = state_ret.shape
    rope = cache_kpe.shape[3]
    assert rope == LANES // 2 and w_uk.shape[3] == LANES and w_uv.shape[3] == LANES
    assert seq % ATTN_Q_TILE == 0 and seq % RET_STEP == 0 and RET_STEP % RET_TILE == 0 and seq % TOKEN_TILE_MLA == 0
    assert streams % RET_SAMPLE_STREAMS == 0 and streams % MLA_SAMPLE_STREAMS == 0

    pos_p = jnp.arange(seq)
    pos_s = past + jnp.arange(dseq)
    mla_cos_p, mla_sin_p = _rope_tables(pos_p, rope, LANES)
    mla_cos_s, mla_sin_s = _rope_tables(jnp.tile(pos_s, streams), rope, LANES)
    ret_tab_p = _ret_tables(pos_p, RET_TILE, ret_heads, dk)
    ret_tab_s = _ret_tables(pos_s, dseq, ret_heads, dk)

    ffn1 = (norm_ffn1[:, None, :], w_ffn1_gate.astype(BF16), w_ffn1_up.astype(BF16), w_ffn1_down.astype(BF16))
    ffn2 = (norm_ffn2[:, None, :], w_ffn2_gate.astype(BF16), w_ffn2_up.astype(BF16), w_ffn2_down.astype(BF16))
    mla = _mla_weights(norm_mix[0::2], w_dq, q_norm, w_uq, w_dkv, kv_norm, w_uk, w_uv, w_o_mla)
    ret = {"norm": norm_mix[:, None, :], "w_in": w_ret_in.astype(BF16),
           "gn": ret_gn.reshape(n_ret, ret_heads, 1, dv), "heads": ret_heads, "dk": dk, "dv": dv}
    w_ret_out_b = w_ret_out.astype(BF16)
    final_g = norm_final[None, :]
    cache_kpe_t = jnp.swapaxes(cache_kpe, 2, 3)

    xp = x_prompt.reshape(batch * seq, d)
    xs = x_sample.reshape(streams * dseq, d)
    lat_p = lat_s = st_p = st_s = None
    for i in range(depth):
        j = i // 2
        if i % 2 == 0:
            heads, scale = mla["heads"], mla["scale"]
            xp = _ffn(xp, i, *ffn1)
            qn, qp, c_p, r_p, knt, v, kpt = _mla_proj(xp, (j, mla, mla_cos_p, mla_sin_p, lat_p, seq))
            lat_p = (c_p, r_p)
            o_p = _mla_attn(qn, qp, knt, kpt, v, batch, seq, scale)
            xs, qn, qp, c_s, r_s, _, _, _ = _ffn(xs, i, *ffn1, mla=(j, mla, mla_cos_s, mla_sin_s, lat_s, None))
            lat_s = (c_s, r_s)
            o_s = _mla_sample(qn, qp, c_s, r_s, cache_ckv, cache_kpe_t, j, mla, streams, dseq, heads, scale)
            out_proj = (mla["w_o"], j)
        else:
            xp = _ffn(xp, i, *ffn1)
            xs = _ffn_stream(xs, i, *ffn1)
            o_p, st_p = _ret(xp.reshape(batch, seq, d), j, i, ret, ret_tab_p, None, st_p, 1, RET_STEP)
            o_s, st_s = _ret(xs.reshape(streams, dseq, d), j, i, ret, ret_tab_s, state_ret, st_s,
                             RET_SAMPLE_STREAMS, dseq)
            o_p = o_p.reshape(batch * seq, -1)
            o_s = o_s.reshape(streams * dseq, -1)
            out_proj = (w_ret_out_b, j)
        fg = final_g if i == depth - 1 else None
        xp = _ffn(xp, i, *ffn2, proj=(o_p,) + out_proj, final_g=fg)
        xs = _ffn_stream(xs, i, *ffn2, proj=(o_s,) + out_proj, final_g=fg)
    n_mla = lat_p[0].shape[0]
    return (xp.reshape(batch, seq, d), xs.reshape(streams, dseq, d),
            lat_p[0].reshape(n_mla, batch, seq, -1), jnp.swapaxes(lat_p[1], 2, 3), st_p,
            lat_s[0].reshape(n_mla, streams, dseq, -1), lat_s[1].reshape(n_mla, streams, dseq, -1), st_s)
```
